```python
import jax, jax.numpy as jnp
from jax import lax
import numpy as np

D_MODEL = 1024
BATCH = 4
SEQ = 8192
DEPTH = 4

SB_HEADS = 8
SB_HEAD_DIM = 64
SB_WIDTH = SB_HEADS * SB_HEAD_DIM
SB_BLOCK = 128
POOL_WINDOWS = (2, 4, 8, 16)
POOL_GROUPS = 4
POOL_GROUP_DIM = 64
POOL_WIDTH = POOL_GROUPS * POOL_GROUP_DIM
GM_GROUPS = 4
GM_GROUP_DIM = 64
GM_WIDTH = GM_GROUPS * GM_GROUP_DIM
GM_CHUNK = 128
N_BRANCH = 3
D_FF = 4 * D_MODEL
RMS_EPS = 1e-6
IN_SIZES = (SB_WIDTH, SB_WIDTH, SB_WIDTH, POOL_WIDTH, GM_WIDTH, GM_WIDTH, N_BRANCH * D_MODEL)
D_IN = sum(IN_SIZES)
IN_SPLITS = tuple(int(s) for s in np.cumsum(IN_SIZES)[:-1])

kernel_name = "hybrid_stickbreak_pool_gmlp_block"


def rms_norm(x, gain):
    xf = x.astype(jnp.float32)
    y = xf * lax.rsqrt(jnp.mean(xf * xf, axis=-1, keepdims=True) + RMS_EPS)
    return (y * gain.astype(jnp.float32)).astype(x.dtype)


def stick_breaking_attention(q, k, v):
    B, S, H, Dh = q.shape
    scale = Dh ** -0.5
    outs = []
    for blk in range(S // SB_BLOCK):
        q0 = blk * SB_BLOCK
        q1 = q0 + SB_BLOCK
        qb = q[:, q0:q1]
        kb = k[:, :q1]
        vb = v[:, :q1]
        z = jnp.einsum('bthd,bshd->bhts', qb, kb).astype(jnp.float32) * scale
        t_idx = q0 + jnp.arange(SB_BLOCK)[:, None]
        s_idx = jnp.arange(q1)[None, :]
        strict = s_idx < t_idx
        log_not = jnp.where(strict, jax.nn.log_sigmoid(-z), 0.0)
        suffix = lax.cumsum(log_not, axis=3, reverse=True) - log_not
        a = jnp.where(strict, jnp.exp(jax.nn.log_sigmoid(z) + suffix), 0.0).astype(v.dtype)
        outs.append(jnp.einsum('bhts,bshd->bthd', a, vb))
    return jnp.concatenate(outs, axis=1)


def multiscale_pool(p, w_pool, pool_scale):
    B, S, _ = p.shape
    pg = p.reshape(B, S, POOL_GROUPS, POOL_GROUP_DIM)
    csum = jnp.cumsum(pg.astype(jnp.float32), axis=1)
    pos = jnp.arange(S, dtype=jnp.float32)
    pooled = []
    for g, w in enumerate(POOL_WINDOWS):
        cg = csum[:, :, g]
        shifted = jnp.pad(cg, ((0, 0), (w, 0), (0, 0)))[:, :S]
        count = jnp.minimum(pos + 1.0, float(w))[None, :, None]
        pooled.append((cg - shifted) / count - pg[:, :, g].astype(jnp.float32))
    pooled = jnp.stack(pooled, axis=2).astype(p.dtype)
    mixed = jnp.einsum('bsgc,gcd->bsgd', pooled, w_pool)
    return mixed.reshape(B, S, POOL_WIDTH) * pool_scale


def chunked_spatial_gating(u, v, gm_gain, w_spatial, b_spatial):
    B, S, _ = u.shape
    u = jax.nn.gelu(u)
    v = rms_norm(jax.nn.gelu(v), gm_gain)
    n_chunks = S // GM_CHUNK
    vc = v.reshape(B, n_chunks, GM_CHUNK, GM_GROUPS, GM_GROUP_DIM)
    causal = jnp.tril(jnp.ones((GM_CHUNK, GM_CHUNK), dtype=bool))
    ws = jnp.where(causal[None], w_spatial, 0.0).astype(v.dtype)
    mixed = jnp.einsum('gtp,bnpgc->bntgc', ws, vc) + b_spatial.T[:, :, None]
    return u * mixed.reshape(B, S, GM_WIDTH)


def setup_inputs(seed: int = 0) -> dict:
    key = jax.random.key(seed)
    ks = jax.random.split(key, 18)

    def nrm(k, shape, scale):
        return jax.random.normal(k, shape, jnp.float32) * scale

    def gain(k, shape):
        return 1.0 + 0.05 * jax.random.normal(k, shape, jnp.float32)

    return {
        "x": nrm(ks[0], (BATCH, SEQ, D_MODEL), 1.0),
        "w_in": nrm(ks[1], (DEPTH, D_MODEL, D_IN), D_MODEL ** -0.5),
        "w_pool": nrm(ks[2], (DEPTH, POOL_GROUPS, POOL_GROUP_DIM, POOL_GROUP_DIM), POOL_GROUP_DIM ** -0.5),
        "pool_scale": gain(ks[3], (DEPTH, POOL_WIDTH)),
        "gm_gain": gain(ks[4], (DEPTH, GM_WIDTH)),
        "w_spatial": nrm(ks[5], (DEPTH, GM_GROUPS, GM_CHUNK, GM_CHUNK), GM_CHUNK ** -0.5),
        "b_spatial": gain(ks[6], (DEPTH, GM_GROUPS, GM_CHUNK)),
        "w_br_sb": nrm(ks[7], (DEPTH, SB_WIDTH, D_MODEL), SB_WIDTH ** -0.5),
        "w_br_pool": nrm(ks[8], (DEPTH, POOL_WIDTH, D_MODEL), POOL_WIDTH ** -0.5),
        "w_br_gm": nrm(ks[9], (DEPTH, GM_WIDTH, D_MODEL), GM_WIDTH ** -0.5),
        "w_out": nrm(ks[10], (DEPTH, D_MODEL, D_MODEL), D_MODEL ** -0.5),
        "g_mix_pre": gain(ks[11], (DEPTH, D_MODEL)),
        "g_mix_post": gain(ks[12], (DEPTH, D_MODEL)),
        "g_ff_pre": gain(ks[13], (DEPTH, D_MODEL)),
        "g_ff_post": gain(ks[14], (DEPTH, D_MODEL)),
        "w_ff_in": nrm(ks[15], (DEPTH, D_MODEL, D_FF), D_MODEL ** -0.5),
        "w_ff_out": nrm(ks[16], (DEPTH, D_FF, D_MODEL), D_FF ** -0.5),
    }


def reference(x, w_in, w_pool, pool_scale, gm_gain, w_spatial, b_spatial, w_br_sb, w_br_pool,
              w_br_gm, w_out, g_mix_pre, g_mix_post, g_ff_pre, g_ff_post, w_ff_in, w_ff_out):
    B, S, D = x.shape
    for l in range(DEPTH):
        h = rms_norm(x, g_mix_pre[l])
        proj = h @ w_in[l]
        q, k, v, p_in, gm_u, gm_v, gate_in = jnp.split(proj, IN_SPLITS, axis=-1)
        o_sb = stick_breaking_attention(q.reshape(B, S, SB_HEADS, SB_HEAD_DIM),
                                        k.reshape(B, S, SB_HEADS, SB_HEAD_DIM),
                                        v.reshape(B, S, SB_HEADS, SB_HEAD_DIM)).reshape(B, S, SB_WIDTH)
        o_pool = multiscale_pool(p_in, w_pool[l], pool_scale[l])
        o_gm = chunked_spatial_gating(gm_u, gm_v, gm_gain[l], w_spatial[l], b_spatial[l])
        gates = jax.nn.sigmoid(gate_in.reshape(B, S, N_BRANCH, D))
        merged = (gates[:, :, 0] * (o_sb @ w_br_sb[l])
                  + gates[:, :, 1] * (o_pool @ w_br_pool[l])
                  + gates[:, :, 2] * (o_gm @ w_br_gm[l]))
        x = x + rms_norm(merged @ w_out[l], g_mix_post[l])
        h = rms_norm(x, g_ff_pre[l])
        ff = jnp.square(jax.nn.relu(h @ w_ff_in[l])) @ w_ff_out[l]
        x = x + rms_norm(ff, g_ff_post[l])
    return x
```

```python
import functools

import jax
import jax.numpy as jnp
from jax import lax
from jax.experimental import pallas as pl
from jax.experimental.pallas import tpu as pltpu

RMS_EPS = 1e-6
SB_HEADS = 8
SB_HEAD_DIM = 64
SB_WIDTH = SB_HEADS * SB_HEAD_DIM
POOL_WINDOWS = (2, 4, 8, 16)
POOL_GROUP_DIM = 64
POOL_WIDTH = len(POOL_WINDOWS) * POOL_GROUP_DIM
POOL_HALO = 16
GM_GROUPS = 4
GM_GROUP_DIM = 64
GM_WIDTH = GM_GROUPS * GM_GROUP_DIM
GM_CHUNK = 128
N_BRANCH = 3

LANES = 128
HEADS_PER_STEP = LANES // SB_HEAD_DIM
ATTN_BLOCK = 128
TOKEN_TILE = 512
FF_CHUNK = 512
VMEM_LIMIT = 56 * 1024 * 1024

LOG_WEIGHT_FLOOR = -104.0

F32 = jnp.float32
BF16 = jnp.bfloat16


def _rms(x, gain):
    return x * lax.rsqrt(jnp.mean(x * x, axis=-1, keepdims=True) + RMS_EPS) * gain


def _dot(a, b):
    return jnp.dot(a, b, preferred_element_type=F32)


def _resident(shape):
    return pl.BlockSpec(shape, lambda *_: (0,) * len(shape), pipeline_mode=pl.Buffered(1))


def _params(n_axes):
    return pltpu.CompilerParams(dimension_semantics=("parallel",) * n_axes,
                                vmem_limit_bytes=VMEM_LIMIT)


def _inproj_kernel(x_ref, g_ref, gmg_ref, w_ref, q_ref, k_ref, v_ref, p_ref, gu_ref, gv_ref,
                   gate_ref, *, d_model):
    h = _rms(x_ref[...], g_ref[...]).astype(BF16)

    def proj(lo, width):
        return _dot(h, w_ref[:, lo:lo + width])

    o = 0
    q_ref[...] = (proj(o, SB_WIDTH) * (SB_HEAD_DIM ** -0.5)).astype(BF16)
    o += SB_WIDTH
    k_ref[...] = proj(o, SB_WIDTH).astype(BF16)
    o += SB_WIDTH
    v_ref[...] = proj(o, SB_WIDTH).astype(BF16)
    o += SB_WIDTH
    p_ref[...] = proj(o, POOL_WIDTH)
    o += POOL_WIDTH
    gu_ref[...] = jax.nn.gelu(proj(o, GM_WIDTH)).astype(BF16)
    o += GM_WIDTH
    gv_ref[...] = _rms(jax.nn.gelu(proj(o, GM_WIDTH)), gmg_ref[...]).astype(BF16)
    o += GM_WIDTH
    for b in range(N_BRANCH):
        gate_ref[:, b * d_model:(b + 1) * d_model] = jax.nn.sigmoid(
            proj(o + b * d_model, d_model)).astype(BF16)


def _inproj(x, g_pre, gm_gain, w_in):
    n, d = x.shape
    tm = TOKEN_TILE
    d_in = w_in.shape[1]

    def rows(width, dtype):
        return pl.BlockSpec((tm, width), lambda i: (i, 0)), jax.ShapeDtypeStruct((n, width), dtype)

    outs = [rows(SB_WIDTH, BF16), rows(SB_WIDTH, BF16), rows(SB_WIDTH, BF16), rows(POOL_WIDTH, F32),
            rows(GM_WIDTH, BF16), rows(GM_WIDTH, BF16), rows(N_BRANCH * d, BF16)]
    return pl.pallas_call(
        functools.partial(_inproj_kernel, d_model=d),
        grid=(n // tm,),
        in_specs=[pl.BlockSpec((tm, d), lambda i: (i, 0)), _resident((1, d)),
                  _resident((1, GM_WIDTH)), _resident((d, d_in))],
        out_specs=[s for s, _ in outs],
        out_shape=[s for _, s in outs],
        compiler_params=_params(1),
        name="inproj",
    )(x, g_pre, gm_gain, w_in)


def _attn_kernel(q_ref, k_ref, v_ref, o_ref):
    t = ATTN_BLOCK
    qi = pl.program_id(2)
    row = lax.broadcasted_iota(jnp.int32, (t, t), 0)
    col = lax.broadcasted_iota(jnp.int32, (t, t), 1)
    strict = col < row
    later = (row > col).astype(BF16)

    outs = []
    for h in range(HEADS_PER_STEP):
        lanes = slice(h * SB_HEAD_DIM, (h + 1) * SB_HEAD_DIM)
        q = q_ref[0, :, lanes]

        def block(j, diagonal, acc, carry, q=q, lanes=lanes):
            start = pl.multiple_of(j * t, t)
            kb = k_ref[0, pl.ds(start, t), lanes]
            vb = v_ref[0, pl.ds(start, t), lanes]
            z = lax.dot_general(q, kb, (((1,), (1,)), ((), ())), preferred_element_type=F32)
            soft = jnp.log(1.0 + jnp.exp(-jnp.abs(z)))
            log_beta = jnp.minimum(z, 0.0) - soft
            log_not = log_beta - z
            if diagonal:
                log_not = jnp.where(strict, log_not, 0.0)
            hi = log_not.astype(BF16)
            lo = (log_not - hi.astype(F32)).astype(BF16)
            suffix = _dot(hi, later) + _dot(lo, later)
            a = jnp.exp(log_beta + suffix + carry)
            if diagonal:
                a = jnp.where(strict, a, 0.0)
            acc = acc + _dot(a.astype(BF16), vb)
            carry = carry + jnp.sum(log_not, axis=-1, keepdims=True)
            return acc, carry

        acc, carry = block(qi, True, jnp.zeros((t, SB_HEAD_DIM), F32), jnp.zeros((t, 1), F32))

        def cond(state):
            j, _, carry = state
            return jnp.logical_and(j >= 0, jnp.max(carry) > LOG_WEIGHT_FLOOR)

        def body(state, block=block):
            j, acc, carry = state
            acc, carry = block(j, False, acc, carry)
            return j - 1, acc, carry

        _, acc, _ = lax.while_loop(cond, body, (qi - 1, acc, carry))
        outs.append(acc)
    o_ref[0] = jnp.concatenate(outs, axis=-1).astype(o_ref.dtype)


def _attention(q, k, v):
    b, s, _ = q.shape
    t = ATTN_BLOCK
    return pl.pallas_call(
        _attn_kernel,
        grid=(b, SB_WIDTH // LANES, s // t),
        in_specs=[pl.BlockSpec((1, t, LANES), lambda bi, hi, qi: (bi, qi, hi)),
                  pl.BlockSpec((1, s, LANES), lambda bi, hi, qi: (bi, 0, hi)),
                  pl.BlockSpec((1, s, LANES), lambda bi, hi, qi: (bi, 0, hi))],
        out_specs=pl.BlockSpec((1, t, LANES), lambda bi, hi, qi: (bi, qi, hi)),
        out_shape=jax.ShapeDtypeStruct((b, s, SB_WIDTH), BF16),
        compiler_params=_params(3),
        name="attention",
    )(q, k, v)


def _merge_kernel(x_ref, osb_ref, p_ref, halo_ref, gu_ref, gv_ref, gate_ref, wpool_ref,
                  pscale_ref, ws_ref, bias_ref, wsb_ref, wpl_ref, wgm_ref, wout_ref, g_ref,
                  o_ref, *, tiles_per_seq, d_model):
    tm = x_ref.shape[0]
    tile_in_seq = pl.program_id(0) % tiles_per_seq
    group = lax.broadcasted_iota(jnp.int32, (1, POOL_WIDTH), 1) // POOL_GROUP_DIM

    halo = jnp.where(tile_in_seq == 0, 0.0, halo_ref[...])
    p = p_ref[...]
    ext = jnp.concatenate([halo, p], axis=0)
    sums = []
    s = ext
    for shift in (1, 2, 4, 8):
        s = s + pltpu.roll(s, shift, 0)
        sums.append(s[POOL_HALO:])
    window_sum = sums[-1]
    for g in range(len(POOL_WINDOWS) - 2, -1, -1):
        window_sum = jnp.where(group == g, sums[g], window_sum)
    pos = (tile_in_seq * tm + lax.broadcasted_iota(jnp.int32, (tm, 1), 0)).astype(F32)
    window = (2 << group).astype(F32)
    count = jnp.minimum(pos + 1.0, window)
    pooled = (window_sum / count - p).astype(BF16)
    pool_feat = (_dot(pooled, wpool_ref[...]) * pscale_ref[...]).astype(BF16)

    r = lax.broadcasted_iota(jnp.int32, (GM_CHUNK, GM_CHUNK), 0)
    c = lax.broadcasted_iota(jnp.int32, (GM_CHUNK, GM_CHUNK), 1)
    ws = [jnp.where(c <= r, ws_ref[g], 0.0).astype(BF16) for g in range(GM_GROUPS)]
    bias = bias_ref[...]
    chunks = []
    for ci in range(tm // GM_CHUNK):
        rows = slice(ci * GM_CHUNK, (ci + 1) * GM_CHUNK)
        vc = gv_ref[rows, :]
        mixed = _dot(ws[GM_GROUPS - 1], vc)
        for g in range(GM_GROUPS - 2, -1, -1):
            mixed = jnp.where(group == g, _dot(ws[g], vc), mixed)
        chunks.append((gu_ref[rows, :].astype(F32) * (mixed + bias)).astype(BF16))
    gm_feat = jnp.concatenate(chunks, axis=0)

    def gate(b):
        return gate_ref[:, b * d_model:(b + 1) * d_model].astype(F32)

    merged = (gate(0) * _dot(osb_ref[...], wsb_ref[...])
              + gate(1) * _dot(pool_feat, wpl_ref[...])
              + gate(2) * _dot(gm_feat, wgm_ref[...]))
    y = _dot(merged.astype(BF16), wout_ref[...])
    o_ref[...] = x_ref[...] + _rms(y, g_ref[...])


def _merge(x, o_sb, p_in, gu, gv, gates, w_pool_bd, pool_scale, w_spatial, bias, w_br_sb, w_br_pool,
           w_br_gm, w_out, g_post, seq):
    n, d = x.shape
    tm = TOKEN_TILE
    halo_blocks = tm // POOL_HALO

    def rows(width):
        return pl.BlockSpec((tm, width), lambda i: (i, 0))

    return pl.pallas_call(
        functools.partial(_merge_kernel, tiles_per_seq=seq // tm, d_model=d),
        grid=(n // tm,),
        in_specs=[rows(d), rows(SB_WIDTH), rows(POOL_WIDTH),
                  pl.BlockSpec((POOL_HALO, POOL_WIDTH),
                               lambda i: (jnp.maximum(i * halo_blocks - 1, 0), 0)),
                  rows(GM_WIDTH), rows(GM_WIDTH), rows(N_BRANCH * d),
                  _resident((POOL_WIDTH, POOL_WIDTH)), _resident((1, POOL_WIDTH)),
                  _resident((GM_GROUPS, GM_CHUNK, GM_CHUNK)), _resident((GM_CHUNK, GM_WIDTH)),
                  _resident((SB_WIDTH, d)), _resident((POOL_WIDTH, d)), _resident((GM_WIDTH, d)),
                  _resident((d, d)), _resident((1, d))],
        out_specs=rows(d),
        out_shape=jax.ShapeDtypeStruct((n, d), F32),
        compiler_params=_params(1),
        name="merge",
    )(x, o_sb, p_in, p_in, gu, gv, gates, w_pool_bd, pool_scale, w_spatial, bias, w_br_sb,
      w_br_pool, w_br_gm, w_out, g_post)


def _ffn_kernel(x_ref, gpre_ref, w1_ref, w2_ref, gpost_ref, o_ref):
    x = x_ref[...]
    h = _rms(x, gpre_ref[...]).astype(BF16)
    d_ff = w1_ref.shape[1]
    ff = jnp.zeros(x.shape, F32)
    for c in range(d_ff // FF_CHUNK):
        cols = slice(c * FF_CHUNK, (c + 1) * FF_CHUNK)
        u = jnp.maximum(_dot(h, w1_ref[:, cols]), 0.0)
        ff = ff + _dot((u * u).astype(BF16), w2_ref[cols, :])
    o_ref[...] = x + _rms(ff, gpost_ref[...])


def _ffn(x, g_pre, w1, w2, g_post):
    n, d = x.shape
    tm = TOKEN_TILE
    d_ff = w1.shape[1]
    return pl.pallas_call(
        _ffn_kernel,
        grid=(n // tm,),
        in_specs=[pl.BlockSpec((tm, d), lambda i: (i, 0)), _resident((1, d)),
                  _resident((d, d_ff)), _resident((d_ff, d)), _resident((1, d))],
        out_specs=pl.BlockSpec((tm, d), lambda i: (i, 0)),
        out_shape=jax.ShapeDtypeStruct((n, d), F32),
        compiler_params=_params(1),
        name="ffn",
    )(x, g_pre, w1, w2, g_post)


def _block_diag(w):
    g, c, _ = w.shape
    eye = jnp.eye(g, dtype=w.dtype)
    return (eye[:, None, :, None] * w[:, :, None, :]).reshape(g * c, g * c)


def kernel(x, w_in, w_pool, pool_scale, gm_gain, w_spatial, b_spatial, w_br_sb, w_br_pool, w_br_gm,
           w_out, g_mix_pre, g_mix_post, g_ff_pre, g_ff_post, w_ff_in, w_ff_out):
    b, s, d = x.shape
    depth = w_in.shape[0]
    assert s % TOKEN_TILE == 0 and s % ATTN_BLOCK == 0 and TOKEN_TILE % GM_CHUNK == 0
    n = b * s
    xt = x.reshape(n, d)
    for l in range(depth):
        q, k, v, p_in, gu, gv, gates = _inproj(
            xt, g_mix_pre[l][None], gm_gain[l][None], w_in[l].astype(BF16))
        o_sb = _attention(q.reshape(b, s, SB_WIDTH), k.reshape(b, s, SB_WIDTH),
                          v.reshape(b, s, SB_WIDTH)).reshape(n, SB_WIDTH)
        bias = jnp.repeat(b_spatial[l].T, GM_GROUP_DIM, axis=1)
        xt = _merge(xt, o_sb, p_in, gu, gv, gates, _block_diag(w_pool[l]).astype(BF16),
                    pool_scale[l][None], w_spatial[l], bias, w_br_sb[l].astype(BF16),
                    w_br_pool[l].astype(BF16), w_br_gm[l].astype(BF16), w_out[l].astype(BF16),
                    g_mix_post[l][None], s)
        xt = _ffn(xt, g_ff_pre[l][None], w_ff_in[l].astype(BF16), w_ff_out[l].astype(BF16),
                  g_ff_post[l][None])
    return xt.reshape(b, s, d)
```

```python
import functools

import jax
import jax.numpy as jnp
from jax import lax
from jax.experimental import pallas as pl
from jax.experimental.pallas import tpu as pltpu

RMS_EPS = 1e-6
SB_HEADS = 8
SB_HEAD_DIM = 64
SB_WIDTH = SB_HEADS * SB_HEAD_DIM
POOL_WINDOWS = (2, 4, 8, 16)
POOL_GROUP_DIM = 64
POOL_WIDTH = len(POOL_WINDOWS) * POOL_GROUP_DIM
POOL_HALO = 16
GM_GROUPS = 4
GM_GROUP_DIM = 64
GM_WIDTH = GM_GROUPS * GM_GROUP_DIM
GM_CHUNK = 128
N_BRANCH = 3

LANES = 128
HEADS_PER_PAIR = LANES // SB_HEAD_DIM
ATTN_PAIRS = 4
ATTN_BLOCK = 256
TOKEN_TILE = 512
FF_CHUNK = 512
VMEM_LIMIT = 56 * 1024 * 1024

LOG_WEIGHT_FLOOR = -104.0

F32 = jnp.float32
BF16 = jnp.bfloat16


def _rms(x, gain):
    return x * lax.rsqrt(jnp.mean(x * x, axis=-1, keepdims=True) + RMS_EPS) * gain


def _dot(a, b):
    return jnp.dot(a, b, preferred_element_type=F32)


def _resident(shape):
    return pl.BlockSpec(shape, lambda *_: (0,) * len(shape), pipeline_mode=pl.Buffered(1))


def _params(n_axes):
    return pltpu.CompilerParams(dimension_semantics=("parallel",) * n_axes,
                                vmem_limit_bytes=VMEM_LIMIT)


def _inproj_kernel(x_ref, g_ref, gmg_ref, w_ref, q_ref, k_ref, v_ref, p_ref, gu_ref, gv_ref,
                   gate_ref, *, d_model):
    h = _rms(x_ref[...], g_ref[...]).astype(BF16)

    def proj(lo, width):
        return _dot(h, w_ref[:, lo:lo + width])

    o = 0
    q_ref[...] = (proj(o, SB_WIDTH) * (SB_HEAD_DIM ** -0.5)).astype(BF16)
    o += SB_WIDTH
    k_ref[...] = proj(o, SB_WIDTH).astype(BF16)
    o += SB_WIDTH
    v_ref[...] = proj(o, SB_WIDTH).astype(BF16)
    o += SB_WIDTH
    p_ref[...] = proj(o, POOL_WIDTH)
    o += POOL_WIDTH
    gu_ref[...] = jax.nn.gelu(proj(o, GM_WIDTH)).astype(BF16)
    o += GM_WIDTH
    gv_ref[...] = _rms(jax.nn.gelu(proj(o, GM_WIDTH)), gmg_ref[...]).astype(BF16)
    o += GM_WIDTH
    for b in range(N_BRANCH):
        gate_ref[:, b * d_model:(b + 1) * d_model] = jax.nn.sigmoid(
            proj(o + b * d_model, d_model)).astype(BF16)


def _inproj(x, g_pre, gm_gain, w_in):
    n, d = x.shape
    tm = TOKEN_TILE
    d_in = w_in.shape[1]

    def rows(width, dtype):
        return pl.BlockSpec((tm, width), lambda i: (i, 0)), jax.ShapeDtypeStruct((n, width), dtype)

    outs = [rows(SB_WIDTH, BF16), rows(SB_WIDTH, BF16), rows(SB_WIDTH, BF16), rows(POOL_WIDTH, F32),
            rows(GM_WIDTH, BF16), rows(GM_WIDTH, BF16), rows(N_BRANCH * d, BF16)]
    return pl.pallas_call(
        functools.partial(_inproj_kernel, d_model=d),
        grid=(n // tm,),
        in_specs=[pl.BlockSpec((tm, d), lambda i: (i, 0)), _resident((1, d)),
                  _resident((1, GM_WIDTH)), _resident((d, d_in))],
        out_specs=[s for s, _ in outs],
        out_shape=[s for _, s in outs],
        compiler_params=_params(1),
        name="inproj",
    )(x, g_pre, gm_gain, w_in)


def _attn_kernel(q_ref, k_ref, v_ref, o_ref, acc_ref, carry_ref):
    t = ATTN_BLOCK
    qi = pl.program_id(2)
    row = lax.broadcasted_iota(jnp.int32, (t, t), 0)
    col = lax.broadcasted_iota(jnp.int32, (t, t), 1)
    strict = col < row
    later = (row > col).astype(BF16)
    head_of_lane = lax.broadcasted_iota(jnp.int32, (1, LANES), 1) // SB_HEAD_DIM
    heads = [(p, h) for p in range(ATTN_PAIRS) for h in range(HEADS_PER_PAIR)]

    def pair_lanes(p):
        return slice(p * LANES, (p + 1) * LANES)

    def head_queries(p, h):
        q = q_ref[0, :, pair_lanes(p)]
        return jnp.where(head_of_lane == h, q, jnp.zeros_like(q))

    q_heads = [head_queries(p, h) for p, h in heads]

    def logits(qh, keys):
        return lax.dot_general(qh, keys, (((1,), (1,)), ((), ())), preferred_element_type=F32)

    def segment(z, carry, diagonal):
        soft = jnp.log(1.0 + jnp.exp(-jnp.abs(z)))
        log_beta = jnp.minimum(z, 0.0) - soft
        log_not = log_beta - z
        if diagonal:
            log_not = jnp.where(strict, log_not, 0.0)
        log_w = log_beta + _dot(log_not.astype(BF16), later)
        if carry is not None:
            log_w = log_w + carry
        a = jnp.exp(log_w)
        if diagonal:
            a = jnp.where(strict, a, 0.0)
        return a, jnp.sum(log_not, axis=-1, keepdims=True)

    def window(ref, start, size):
        return [ref[0, pl.ds(start, size), pair_lanes(p)] for p in range(ATTN_PAIRS)]

    def finish():
        for p in range(ATTN_PAIRS):
            base = p * HEADS_PER_PAIR
            o_ref[0, :, pair_lanes(p)] = jnp.where(
                head_of_lane == 0, acc_ref[base], acc_ref[base + 1]).astype(o_ref.dtype)

    @pl.when(qi == 0)
    def _first_block():
        keys = window(k_ref, 0, t)
        vals = window(v_ref, 0, t)
        for i, (p, _) in enumerate(heads):
            a, _ = segment(logits(q_heads[i], keys[p]), None, True)
            acc_ref[i] = _dot(a.astype(BF16), vals[p])
        finish()

    @pl.when(qi > 0)
    def _later_blocks():
        start = pl.multiple_of((qi - 1) * t, t)
        keys = window(k_ref, start, 2 * t)
        vals = window(v_ref, start, 2 * t)
        z = [logits(q_heads[i], keys[p]) for i, (p, _) in enumerate(heads)]
        diag = [segment(zi[:, t:], None, True) for zi in z]
        prev = [segment(zi[:, :t], d[1], False) for zi, d in zip(z, diag)]
        worst = None
        for i, (p, _) in enumerate(heads):
            a = jnp.concatenate([prev[i][0], diag[i][0]], axis=1).astype(BF16)
            acc_ref[i] = _dot(a, vals[p])
            carry = diag[i][1] + prev[i][1]
            carry_ref[i] = carry
            top = jnp.max(carry)
            worst = top if worst is None else jnp.maximum(worst, top)

        def cond(state):
            j, worst = state
            return jnp.logical_and(j >= 0, worst > LOG_WEIGHT_FLOOR)

        def body(state):
            j, _ = state
            start = pl.multiple_of(j * t, t)
            keys = window(k_ref, start, t)
            vals = window(v_ref, start, t)
            worst = None
            for i, (p, _) in enumerate(heads):
                carry = carry_ref[i]
                a, total = segment(logits(q_heads[i], keys[p]), carry, False)
                acc_ref[i] += _dot(a.astype(BF16), vals[p])
                carry = carry + total
                carry_ref[i] = carry
                top = jnp.max(carry)
                worst = top if worst is None else jnp.maximum(worst, top)
            return j - 1, worst

        lax.while_loop(cond, body, (qi - 2, worst))
        finish()


def _attention(q, k, v):
    b, s, _ = q.shape
    t = ATTN_BLOCK
    width = ATTN_PAIRS * LANES
    n_heads = ATTN_PAIRS * HEADS_PER_PAIR
    return pl.pallas_call(
        _attn_kernel,
        grid=(b, SB_WIDTH // width, s // t),
        in_specs=[pl.BlockSpec((1, t, width), lambda bi, hi, qi: (bi, qi, hi)),
                  pl.BlockSpec((1, s, width), lambda bi, hi, qi: (bi, 0, hi)),
                  pl.BlockSpec((1, s, width), lambda bi, hi, qi: (bi, 0, hi))],
        out_specs=pl.BlockSpec((1, t, width), lambda bi, hi, qi: (bi, qi, hi)),
        out_shape=jax.ShapeDtypeStruct((b, s, SB_WIDTH), BF16),
        scratch_shapes=[pltpu.VMEM((n_heads, t, LANES), F32),
                        pltpu.VMEM((n_heads, t, 1), F32)],
        compiler_params=_params(3),
        name="attention",
    )(q, k, v)


def _merge_kernel(x_ref, osb_ref, p_ref, halo_ref, gu_ref, gv_ref, gate_ref, wpool_ref,
                  pscale_ref, ws_ref, bias_ref, wsb_ref, wpl_ref, wgm_ref, wout_ref, g_ref,
                  o_ref, *, tiles_per_seq, d_model):
    tm = x_ref.shape[0]
    tile_in_seq = pl.program_id(0) % tiles_per_seq
    group = lax.broadcasted_iota(jnp.int32, (1, POOL_WIDTH), 1) // POOL_GROUP_DIM

    halo = jnp.where(tile_in_seq == 0, 0.0, halo_ref[...])
    p = p_ref[...]
    ext = jnp.concatenate([halo, p], axis=0)
    sums = []
    s = ext
    for shift in (1, 2, 4, 8):
        s = s + pltpu.roll(s, shift, 0)
        sums.append(s[POOL_HALO:])
    window_sum = sums[-1]
    for g in range(len(POOL_WINDOWS) - 2, -1, -1):
        window_sum = jnp.where(group == g, sums[g], window_sum)
    pos = (tile_in_seq * tm + lax.broadcasted_iota(jnp.int32, (tm, 1), 0)).astype(F32)
    window = (2 << group).astype(F32)
    count = jnp.minimum(pos + 1.0, window)
    pooled = (window_sum / count - p).astype(BF16)
    pool_feat = (_dot(pooled, wpool_ref[...]) * pscale_ref[...]).astype(BF16)

    r = lax.broadcasted_iota(jnp.int32, (GM_CHUNK, GM_CHUNK), 0)
    c = lax.broadcasted_iota(jnp.int32, (GM_CHUNK, GM_CHUNK), 1)
    ws = [jnp.where(c <= r, ws_ref[g], 0.0).astype(BF16) for g in range(GM_GROUPS)]
    bias = bias_ref[...]
    chunks = []
    for ci in range(tm // GM_CHUNK):
        rows = slice(ci * GM_CHUNK, (ci + 1) * GM_CHUNK)
        vc = gv_ref[rows, :]
        mixed = _dot(ws[GM_GROUPS - 1], vc)
        for g in range(GM_GROUPS - 2, -1, -1):
            mixed = jnp.where(group == g, _dot(ws[g], vc), mixed)
        chunks.append((gu_ref[rows, :].astype(F32) * (mixed + bias)).astype(BF16))
    gm_feat = jnp.concatenate(chunks, axis=0)

    def gate(b):
        return gate_ref[:, b * d_model:(b + 1) * d_model].astype(F32)

    merged = (gate(0) * _dot(osb_ref[...], wsb_ref[...])
              + gate(1) * _dot(pool_feat, wpl_ref[...])
              + gate(2) * _dot(gm_feat, wgm_ref[...]))
    y = _dot(merged.astype(BF16), wout_ref[...])
    o_ref[...] = x_ref[...] + _rms(y, g_ref[...])


def _merge(x, o_sb, p_in, gu, gv, gates, w_pool_bd, pool_scale, w_spatial, bias, w_br_sb, w_br_pool,
           w_br_gm, w_out, g_post, seq):
    n, d = x.shape
    tm = TOKEN_TILE
    halo_blocks = tm // POOL_HALO

    def rows(width):
        return pl.BlockSpec((tm, width), lambda i: (i, 0))

    return pl.pallas_call(
        functools.partial(_merge_kernel, tiles_per_seq=seq // tm, d_model=d),
        grid=(n // tm,),
        in_specs=[rows(d), rows(SB_WIDTH), rows(POOL_WIDTH),
                  pl.BlockSpec((POOL_HALO, POOL_WIDTH),
                               lambda i: (jnp.maximum(i * halo_blocks - 1, 0), 0)),
                  rows(GM_WIDTH), rows(GM_WIDTH), rows(N_BRANCH * d),
                  _resident((POOL_WIDTH, POOL_WIDTH)), _resident((1, POOL_WIDTH)),
                  _resident((GM_GROUPS, GM_CHUNK, GM_CHUNK)), _resident((GM_CHUNK, GM_WIDTH)),
                  _resident((SB_WIDTH, d)), _resident((POOL_WIDTH, d)), _resident((GM_WIDTH, d)),
                  _resident((d, d)), _resident((1, d))],
        out_specs=rows(d),
        out_shape=jax.ShapeDtypeStruct((n, d), F32),
        compiler_params=_params(1),
        name="merge",
    )(x, o_sb, p_in, p_in, gu, gv, gates, w_pool_bd, pool_scale, w_spatial, bias, w_br_sb,
      w_br_pool, w_br_gm, w_out, g_post)


def _ffn_kernel(x_ref, gpre_ref, w1_ref, w2_ref, gpost_ref, o_ref):
    x = x_ref[...]
    h = _rms(x, gpre_ref[...]).astype(BF16)
    d_ff = w1_ref.shape[1]
    ff = jnp.zeros(x.shape, F32)
    for c in range(d_ff // FF_CHUNK):
        cols = slice(c * FF_CHUNK, (c + 1) * FF_CHUNK)
        u = jnp.maximum(_dot(h, w1_ref[:, cols]), 0.0)
        ff = ff + _dot((u * u).astype(BF16), w2_ref[cols, :])
    o_ref[...] = x + _rms(ff, gpost_ref[...])


def _ffn(x, g_pre, w1, w2, g_post):
    n, d = x.shape
    tm = TOKEN_TILE
    d_ff = w1.shape[1]
    return pl.pallas_call(
        _ffn_kernel,
        grid=(n // tm,),
        in_specs=[pl.BlockSpec((tm, d), lambda i: (i, 0)), _resident((1, d)),
                  _resident((d, d_ff)), _resident((d_ff, d)), _resident((1, d))],
        out_specs=pl.BlockSpec((tm, d), lambda i: (i, 0)),
        out_shape=jax.ShapeDtypeStruct((n, d), F32),
        compiler_params=_params(1),
        name="ffn",
    )(x, g_pre, w1, w2, g_post)


def _block_diag(w):
    g, c, _ = w.shape
    eye = jnp.eye(g, dtype=w.dtype)
    return (eye[:, None, :, None] * w[:, :, None, :]).reshape(g * c, g * c)


def kernel(x, w_in, w_pool, pool_scale, gm_gain, w_spatial, b_spatial, w_br_sb, w_br_pool, w_br_gm,
           w_out, g_mix_pre, g_mix_post, g_ff_pre, g_ff_post, w_ff_in, w_ff_out):
    b, s, d = x.shape
    depth = w_in.shape[0]
    assert s % TOKEN_TILE == 0 and s % ATTN_BLOCK == 0 and TOKEN_TILE % GM_CHUNK == 0
    n = b * s
    xt = x.reshape(n, d)
    for l in range(depth):
        q, k, v, p_in, gu, gv, gates = _inproj(
            xt, g_mix_pre[l][None], gm_gain[l][None], w_in[l].astype(BF16))
        o_sb = _attention(q.reshape(b, s, SB_WIDTH), k.reshape(b, s, SB_WIDTH),
                          v.reshape(b, s, SB_WIDTH)).reshape(n, SB_WIDTH)
        bias = jnp.repeat(b_spatial[l].T, GM_GROUP_DIM, axis=1)
        xt = _merge(xt, o_sb, p_in, gu, gv, gates, _block_diag(w_pool[l]).astype(BF16),
                    pool_scale[l][None], w_spatial[l], bias, w_br_sb[l].astype(BF16),
                    w_br_pool[l].astype(BF16), w_br_gm[l].astype(BF16), w_out[l].astype(BF16),
                    g_mix_post[l][None], s)
        xt = _ffn(xt, g_ff_pre[l][None], w_ff_in[l].astype(BF16), w_ff_out[l].astype(BF16),
                  g_ff_post[l][None])
    return xt.reshape(b, s, d)
```

```python
import functools

import jax
import jax.numpy as jnp
from jax import lax
from jax.experimental import pallas as pl
from jax.experimental.pallas import tpu as pltpu

RMS_EPS = 1e-6
SB_HEADS = 8
SB_HEAD_DIM = 64
SB_WIDTH = SB_HEADS * SB_HEAD_DIM
POOL_WINDOWS = (2, 4, 8, 16)
POOL_GROUP_DIM = 64
POOL_WIDTH = len(POOL_WINDOWS) * POOL_GROUP_DIM
POOL_HALO = 16
GM_GROUPS = 4
GM_GROUP_DIM = 64
GM_WIDTH = GM_GROUPS * GM_GROUP_DIM
GM_CHUNK = 128
N_BRANCH = 3

LANES = 128
HEADS_PER_PAIR = LANES // SB_HEAD_DIM
ATTN_PAIRS = 4
ATTN_BLOCK = 256
TOKEN_TILE = 1024
SUB_TILE = 512
FF_CHUNK = 512
VMEM_LIMIT = 56 * 1024 * 1024

LOG_WEIGHT_FLOOR = -104.0

F32 = jnp.float32
BF16 = jnp.bfloat16


def _rms(x, gain):
    return x * lax.rsqrt(jnp.mean(x * x, axis=-1, keepdims=True) + RMS_EPS) * gain


def _dot(a, b):
    return jnp.dot(a, b, preferred_element_type=F32)


def _resident(shape):
    return pl.BlockSpec(shape, lambda *_: (0,) * len(shape), pipeline_mode=pl.Buffered(1))


def _params(n_axes):
    return pltpu.CompilerParams(dimension_semantics=("parallel",) * n_axes,
                                vmem_limit_bytes=VMEM_LIMIT)


def _inproj_kernel(x_ref, g_ref, gmg_ref, w_ref, q_ref, k_ref, v_ref, p_ref, gu_ref, gv_ref,
                   gate_ref, *, d_model):
    h = _rms(x_ref[...], g_ref[...]).astype(BF16)

    def proj(lo, width):
        return _dot(h, w_ref[:, lo:lo + width])

    q_at, k_at, v_at = 0, SB_WIDTH, 2 * SB_WIDTH
    p_at = 3 * SB_WIDTH
    gu_at = p_at + POOL_WIDTH
    gv_at = gu_at + GM_WIDTH
    gate_at = gv_at + GM_WIDTH
    for b in range(N_BRANCH):
        gate_ref[:, b * d_model:(b + 1) * d_model] = jax.nn.sigmoid(
            proj(gate_at + b * d_model, d_model)).astype(BF16)
    gu_ref[...] = jax.nn.gelu(proj(gu_at, GM_WIDTH)).astype(BF16)
    gv_ref[...] = _rms(jax.nn.gelu(proj(gv_at, GM_WIDTH)), gmg_ref[...]).astype(BF16)
    p_ref[...] = proj(p_at, POOL_WIDTH)
    q_ref[...] = (proj(q_at, SB_WIDTH) * (SB_HEAD_DIM ** -0.5)).astype(BF16)
    k_ref[...] = proj(k_at, SB_WIDTH).astype(BF16)
    v_ref[...] = proj(v_at, SB_WIDTH).astype(BF16)


def _inproj(x, g_pre, gm_gain, w_in):
    n, d = x.shape
    tm = TOKEN_TILE
    d_in = w_in.shape[1]

    def rows(width, dtype):
        return pl.BlockSpec((tm, width), lambda i: (i, 0)), jax.ShapeDtypeStruct((n, width), dtype)

    outs = [rows(SB_WIDTH, BF16), rows(SB_WIDTH, BF16), rows(SB_WIDTH, BF16), rows(POOL_WIDTH, F32),
            rows(GM_WIDTH, BF16), rows(GM_WIDTH, BF16), rows(N_BRANCH * d, BF16)]
    return pl.pallas_call(
        functools.partial(_inproj_kernel, d_model=d),
        grid=(n // tm,),
        in_specs=[pl.BlockSpec((tm, d), lambda i: (i, 0)), _resident((1, d)),
                  _resident((1, GM_WIDTH)), _resident((d, d_in))],
        out_specs=[s for s, _ in outs],
        out_shape=[s for _, s in outs],
        compiler_params=_params(1),
        name="inproj",
    )(x, g_pre, gm_gain, w_in)


def _attn_kernel(q_ref, k_ref, v_ref, o_ref, acc_ref, carry_ref):
    t = ATTN_BLOCK
    qi = pl.program_id(2)
    row = lax.broadcasted_iota(jnp.int32, (t, t), 0)
    col = lax.broadcasted_iota(jnp.int32, (t, t), 1)
    strict = col < row
    later = (row > col).astype(BF16)
    head_of_lane = lax.broadcasted_iota(jnp.int32, (1, LANES), 1) // SB_HEAD_DIM
    heads = [(p, h) for p in range(ATTN_PAIRS) for h in range(HEADS_PER_PAIR)]

    def pair_lanes(p):
        return slice(p * LANES, (p + 1) * LANES)

    def head_queries(p, h):
        q = q_ref[0, :, pair_lanes(p)]
        return jnp.where(head_of_lane == h, q, jnp.zeros_like(q))

    q_heads = [head_queries(p, h) for p, h in heads]

    def logits(qh, keys):
        return lax.dot_general(qh, keys, (((1,), (1,)), ((), ())), preferred_element_type=F32)

    def segment(z, carry, diagonal):
        soft = jnp.log(1.0 + jnp.exp(-jnp.abs(z)))
        log_beta = jnp.minimum(z, 0.0) - soft
        log_not = log_beta - z
        if diagonal:
            log_not = jnp.where(strict, log_not, 0.0)
        log_w = log_beta + _dot(log_not.astype(BF16), later)
        if carry is not None:
            log_w = log_w + carry
        a = jnp.exp(log_w)
        if diagonal:
            a = jnp.where(strict, a, 0.0)
        return a.astype(BF16), jnp.sum(log_not, axis=-1, keepdims=True)

    def window(ref, start, size):
        return [ref[0, pl.ds(start, size), pair_lanes(p)] for p in range(ATTN_PAIRS)]

    def finish():
        for p in range(ATTN_PAIRS):
            base = p * HEADS_PER_PAIR
            o_ref[0, :, pair_lanes(p)] = jnp.where(
                head_of_lane == 0, acc_ref[base], acc_ref[base + 1]).astype(o_ref.dtype)

    @pl.when(qi == 0)
    def _first_block():
        keys = window(k_ref, 0, t)
        vals = window(v_ref, 0, t)
        for i, (p, _) in enumerate(heads):
            a, _ = segment(logits(q_heads[i], keys[p]), None, True)
            acc_ref[i] = _dot(a, vals[p])
        finish()

    @pl.when(qi > 0)
    def _later_blocks():
        start = pl.multiple_of((qi - 1) * t, t)
        keys = window(k_ref, start, 2 * t)
        vals = window(v_ref, start, 2 * t)
        z = [logits(q_heads[i], keys[p]) for i, (p, _) in enumerate(heads)]
        diag = [segment(zi[:, t:], None, True) for zi in z]
        prev = [segment(zi[:, :t], d[1], False) for zi, d in zip(z, diag)]
        worst = [None] * ATTN_PAIRS
        for i, (p, _) in enumerate(heads):
            acc_ref[i] = _dot(jnp.concatenate([prev[i][0], diag[i][0]], axis=1), vals[p])
            carry = diag[i][1] + prev[i][1]
            carry_ref[i] = carry
            worst[p] = carry if worst[p] is None else jnp.maximum(worst[p], carry)

        def cond(state):
            j, worst = state[0], functools.reduce(jnp.maximum, state[1:])
            return jnp.logical_and(j >= 0, worst > LOG_WEIGHT_FLOOR)

        def body(state):
            j = state[0]
            start = pl.multiple_of(j * t, t)
            new_state = [j - 1]
            for p in range(ATTN_PAIRS):
                def walk_pair(p=p):
                    keys = k_ref[0, pl.ds(start, t), pair_lanes(p)]
                    vals = v_ref[0, pl.ds(start, t), pair_lanes(p)]
                    worst = None
                    for h in range(HEADS_PER_PAIR):
                        i = p * HEADS_PER_PAIR + h
                        carry = carry_ref[i]
                        a, total = segment(logits(q_heads[i], keys), carry, False)
                        acc_ref[i] += _dot(a, vals)
                        carry = carry + total
                        carry_ref[i] = carry
                        worst = carry if worst is None else jnp.maximum(worst, carry)
                    return jnp.max(worst)

                new_state.append(lax.cond(state[1 + p] > LOG_WEIGHT_FLOOR, walk_pair,
                                          lambda p=p: state[1 + p]))
            return tuple(new_state)

        lax.while_loop(cond, body, (qi - 2,) + tuple(jnp.max(w) for w in worst))
        finish()


def _attention(q, k, v):
    b, s, _ = q.shape
    t = ATTN_BLOCK
    width = ATTN_PAIRS * LANES
    n_heads = ATTN_PAIRS * HEADS_PER_PAIR
    return pl.pallas_call(
        _attn_kernel,
        grid=(b, SB_WIDTH // width, s // t),
        in_specs=[pl.BlockSpec((1, t, width), lambda bi, hi, qi: (bi, qi, hi)),
                  pl.BlockSpec((1, s, width), lambda bi, hi, qi: (bi, 0, hi)),
                  pl.BlockSpec((1, s, width), lambda bi, hi, qi: (bi, 0, hi))],
        out_specs=pl.BlockSpec((1, t, width), lambda bi, hi, qi: (bi, qi, hi)),
        out_shape=jax.ShapeDtypeStruct((b, s, SB_WIDTH), BF16),
        scratch_shapes=[pltpu.VMEM((n_heads, t, LANES), F32),
                        pltpu.VMEM((n_heads, t, 1), F32)],
        compiler_params=_params(3),
        name="attention",
    )(q, k, v)


def _merge_kernel(x_ref, osb_ref, p_ref, halo_ref, gu_ref, gv_ref, gate_ref, wpool_ref,
                  pscale_ref, ws_ref, bias_ref, wsb_ref, wpl_ref, wgm_ref, wout_ref, g_ref,
                  o_ref, *, tiles_per_seq, d_model):
    tm = x_ref.shape[0]
    tile_in_seq = pl.program_id(0) % tiles_per_seq
    group = lax.broadcasted_iota(jnp.int32, (1, POOL_WIDTH), 1) // POOL_GROUP_DIM

    halo = jnp.where(tile_in_seq == 0, 0.0, halo_ref[...])
    p = p_ref[...]
    ext = jnp.concatenate([halo, p], axis=0)
    sums = []
    s = ext
    for shift in (1, 2, 4, 8):
        s = s + pltpu.roll(s, shift, 0)
        sums.append(s[POOL_HALO:])
    window_sum = sums[-1]
    for g in range(len(POOL_WINDOWS) - 2, -1, -1):
        window_sum = jnp.where(group == g, sums[g], window_sum)
    pos = (tile_in_seq * tm + lax.broadcasted_iota(jnp.int32, (tm, 1), 0)).astype(F32)
    window = (2 << group).astype(F32)
    count = jnp.minimum(pos + 1.0, window)
    pooled = (window_sum / count - p).astype(BF16)
    pool_feat = (_dot(pooled, wpool_ref[...]) * pscale_ref[...]).astype(BF16)

    r = lax.broadcasted_iota(jnp.int32, (GM_CHUNK, GM_CHUNK), 0)
    c = lax.broadcasted_iota(jnp.int32, (GM_CHUNK, GM_CHUNK), 1)
    ws = [jnp.where(c <= r, ws_ref[g], 0.0).astype(BF16) for g in range(GM_GROUPS)]
    bias = bias_ref[...]
    chunks = []
    for ci in range(tm // GM_CHUNK):
        rows = slice(ci * GM_CHUNK, (ci + 1) * GM_CHUNK)
        vc = gv_ref[rows, :]
        mixed = _dot(ws[GM_GROUPS - 1], vc)
        for g in range(GM_GROUPS - 2, -1, -1):
            mixed = jnp.where(group == g, _dot(ws[g], vc), mixed)
        chunks.append((gu_ref[rows, :].astype(F32) * (mixed + bias)).astype(BF16))
    gm_feat = jnp.concatenate(chunks, axis=0)

    def gate(b):
        return gate_ref[:, b * d_model:(b + 1) * d_model].astype(F32)

    merged = (gate(0) * _dot(osb_ref[...], wsb_ref[...])
              + gate(1) * _dot(pool_feat, wpl_ref[...])
              + gate(2) * _dot(gm_feat, wgm_ref[...]))
    y = _dot(merged.astype(BF16), wout_ref[...])
    o_ref[...] = x_ref[...] + _rms(y, g_ref[...])


def _merge(x, o_sb, p_in, gu, gv, gates, w_pool_bd, pool_scale, w_spatial, bias, w_br_sb, w_br_pool,
           w_br_gm, w_out, g_post, seq):
    n, d = x.shape
    tm = TOKEN_TILE
    halo_blocks = tm // POOL_HALO

    def rows(width):
        return pl.BlockSpec((tm, width), lambda i: (i, 0))

    return pl.pallas_call(
        functools.partial(_merge_kernel, tiles_per_seq=seq // tm, d_model=d),
        grid=(n // tm,),
        in_specs=[rows(d), rows(SB_WIDTH), rows(POOL_WIDTH),
                  pl.BlockSpec((POOL_HALO, POOL_WIDTH),
                               lambda i: (jnp.maximum(i * halo_blocks - 1, 0), 0)),
                  rows(GM_WIDTH), rows(GM_WIDTH), rows(N_BRANCH * d),
                  _resident((POOL_WIDTH, POOL_WIDTH)), _resident((1, POOL_WIDTH)),
                  _resident((GM_GROUPS, GM_CHUNK, GM_CHUNK)), _resident((GM_CHUNK, GM_WIDTH)),
                  _resident((SB_WIDTH, d)), _resident((POOL_WIDTH, d)), _resident((GM_WIDTH, d)),
                  _resident((d, d)), _resident((1, d))],
        out_specs=rows(d),
        out_shape=jax.ShapeDtypeStruct((n, d), F32),
        compiler_params=_params(1),
        name="merge",
    )(x, o_sb, p_in, p_in, gu, gv, gates, w_pool_bd, pool_scale, w_spatial, bias, w_br_sb,
      w_br_pool, w_br_gm, w_out, g_post)


def _ffn_kernel(x_ref, gpre_ref, w1_ref, w2_ref, gpost_ref, o_ref):
    d_ff = w1_ref.shape[1]
    subs = [slice(r, r + SUB_TILE) for r in range(0, x_ref.shape[0], SUB_TILE)]
    h = [_rms(x_ref[rows, :], gpre_ref[...]).astype(BF16) for rows in subs]
    ff = [jnp.zeros((SUB_TILE, x_ref.shape[1]), F32) for _ in subs]
    for c in range(d_ff // FF_CHUNK):
        cols = slice(c * FF_CHUNK, (c + 1) * FF_CHUNK)
        for i in range(len(subs)):
            u = jnp.maximum(_dot(h[i], w1_ref[:, cols]), 0.0)
            ff[i] = ff[i] + _dot((u * u).astype(BF16), w2_ref[cols, :])
    for i, rows in enumerate(subs):
        o_ref[rows, :] = x_ref[rows, :] + _rms(ff[i], gpost_ref[...])


def _ffn(x, g_pre, w1, w2, g_post):
    n, d = x.shape
    tm = TOKEN_TILE
    d_ff = w1.shape[1]
    return pl.pallas_call(
        _ffn_kernel,
        grid=(n // tm,),
        in_specs=[pl.BlockSpec((tm, d), lambda i: (i, 0)), _resident((1, d)),
                  _resident((d, d_ff)), _resident((d_ff, d)), _resident((1, d))],
        out_specs=pl.BlockSpec((tm, d), lambda i: (i, 0)),
        out_shape=jax.ShapeDtypeStruct((n, d), F32),
        compiler_params=_params(1),
        name="ffn",
    )(x, g_pre, w1, w2, g_post)


def _block_diag(w):
    g, c, _ = w.shape
    eye = jnp.eye(g, dtype=w.dtype)
    return (eye[:, None, :, None] * w[:, :, None, :]).reshape(g * c, g * c)


def kernel(x, w_in, w_pool, pool_scale, gm_gain, w_spatial, b_spatial, w_br_sb, w_br_pool, w_br_gm,
           w_out, g_mix_pre, g_mix_post, g_ff_pre, g_ff_post, w_ff_in, w_ff_out):
    b, s, d = x.shape
    depth = w_in.shape[0]
    assert s % TOKEN_TILE == 0 and s % ATTN_BLOCK == 0
    assert TOKEN_TILE % GM_CHUNK == 0 and TOKEN_TILE % SUB_TILE == 0
    n = b * s
    xt = x.reshape(n, d)
    for l in range(depth):
        q, k, v, p_in, gu, gv, gates = _inproj(
            xt, g_mix_pre[l][None], gm_gain[l][None], w_in[l].astype(BF16))
        o_sb = _attention(q.reshape(b, s, SB_WIDTH), k.reshape(b, s, SB_WIDTH),
                          v.reshape(b, s, SB_WIDTH)).reshape(n, SB_WIDTH)
        bias = jnp.repeat(b_spatial[l].T, GM_GROUP_DIM, axis=1)
        xt = _merge(xt, o_sb, p_in, gu, gv, gates, _block_diag(w_pool[l]).astype(BF16),
                    pool_scale[l][None], w_spatial[l], bias, w_br_sb[l].astype(BF16),
                    w_br_pool[l].astype(BF16), w_br_gm[l].astype(BF16), w_out[l].astype(BF16),
                    g_mix_post[l][None], s)
        xt = _ffn(xt, g_ff_pre[l][None], w_ff_in[l].astype(BF16), w_ff_out[l].astype(BF16),
                  g_ff_post[l][None])
    return xt.reshape(b, s, d)
```

```python
import functools

import jax
import jax.numpy as jnp
from jax import lax
from jax.experimental import pallas as pl
from jax.experimental.pallas import tpu as pltpu

RMS_EPS = 1e-6
SB_HEADS = 8
SB_HEAD_DIM = 64
SB_WIDTH = SB_HEADS * SB_HEAD_DIM
POOL_WINDOWS = (2, 4, 8, 16)
POOL_GROUP_DIM = 64
POOL_WIDTH = len(POOL_WINDOWS) * POOL_GROUP_DIM
POOL_HALO = 16
GM_GROUPS = 4
GM_GROUP_DIM = 64
GM_WIDTH = GM_GROUPS * GM_GROUP_DIM
GM_CHUNK = 128
N_BRANCH = 3

LANES = 128
HEADS_PER_PAIR = LANES // SB_HEAD_DIM
ATTN_PAIRS = 4
ATTN_BLOCK = 256
TOKEN_TILE = 1024
SUB_TILE = 512
FF_CHUNK = 512
VMEM_LIMIT = 56 * 1024 * 1024

LOG_WEIGHT_FLOOR = -104.0

F32 = jnp.float32
BF16 = jnp.bfloat16


def _rms(x, gain):
    return x * lax.rsqrt(jnp.mean(x * x, axis=-1, keepdims=True) + RMS_EPS) * gain


def _dot(a, b):
    return jnp.dot(a, b, preferred_element_type=F32)


def _resident(shape):
    return pl.BlockSpec(shape, lambda *_: (0,) * len(shape), pipeline_mode=pl.Buffered(1))


def _params(n_axes):
    return pltpu.CompilerParams(dimension_semantics=("parallel",) * n_axes,
                                vmem_limit_bytes=VMEM_LIMIT)


def _inproj_kernel(x_ref, g_ref, gmg_ref, w_ref, q_ref, k_ref, v_ref, p_ref, gu_ref, gv_ref,
                   gate_ref, *, d_model):
    h = _rms(x_ref[...], g_ref[...]).astype(BF16)

    def proj(lo, width):
        return _dot(h, w_ref[:, lo:lo + width])

    q_at, k_at, v_at = 0, SB_WIDTH, 2 * SB_WIDTH
    p_at = 3 * SB_WIDTH
    gu_at = p_at + POOL_WIDTH
    gv_at = gu_at + GM_WIDTH
    gate_at = gv_at + GM_WIDTH
    for b in range(N_BRANCH):
        gate_ref[:, b * d_model:(b + 1) * d_model] = jax.nn.sigmoid(
            proj(gate_at + b * d_model, d_model)).astype(BF16)
    gu_ref[...] = jax.nn.gelu(proj(gu_at, GM_WIDTH)).astype(BF16)
    gv_ref[...] = _rms(jax.nn.gelu(proj(gv_at, GM_WIDTH)), gmg_ref[...]).astype(BF16)
    p_ref[...] = proj(p_at, POOL_WIDTH)
    q_ref[...] = (proj(q_at, SB_WIDTH) * (SB_HEAD_DIM ** -0.5)).astype(BF16)
    k_ref[...] = proj(k_at, SB_WIDTH).astype(BF16)
    v_ref[...] = proj(v_at, SB_WIDTH).astype(BF16)


def _inproj(x, g_pre, gm_gain, w_in):
    n, d = x.shape
    tm = TOKEN_TILE
    d_in = w_in.shape[1]

    def rows(width, dtype):
        return pl.BlockSpec((tm, width), lambda i: (i, 0)), jax.ShapeDtypeStruct((n, width), dtype)

    outs = [rows(SB_WIDTH, BF16), rows(SB_WIDTH, BF16), rows(SB_WIDTH, BF16), rows(POOL_WIDTH, F32),
            rows(GM_WIDTH, BF16), rows(GM_WIDTH, BF16), rows(N_BRANCH * d, BF16)]
    return pl.pallas_call(
        functools.partial(_inproj_kernel, d_model=d),
        grid=(n // tm,),
        in_specs=[pl.BlockSpec((tm, d), lambda i: (i, 0)), _resident((1, d)),
                  _resident((1, GM_WIDTH)), _resident((d, d_in))],
        out_specs=[s for s, _ in outs],
        out_shape=[s for _, s in outs],
        compiler_params=_params(1),
        name="inproj",
    )(x, g_pre, gm_gain, w_in)


def _attn_kernel(q_ref, k_ref, v_ref, o_ref, acc_ref, carry_ref):
    t = ATTN_BLOCK
    half = t // 2
    qi = pl.program_id(2)
    row = lax.broadcasted_iota(jnp.int32, (t, t), 0)
    col = lax.broadcasted_iota(jnp.int32, (t, t), 1)
    strict = col < row
    later = (row > col).astype(BF16)
    head_of_lane = lax.broadcasted_iota(jnp.int32, (1, LANES), 1) // SB_HEAD_DIM
    heads = [(p, h) for p in range(ATTN_PAIRS) for h in range(HEADS_PER_PAIR)]

    def pair_lanes(p):
        return slice(p * LANES, (p + 1) * LANES)

    def head_queries(p, h):
        q = q_ref[0, :, pair_lanes(p)]
        return jnp.where(head_of_lane == h, q, jnp.zeros_like(q))

    q_heads = [head_queries(p, h) for p, h in heads]

    def logits(qh, keys):
        return lax.dot_general(qh, keys, (((1,), (1,)), ((), ())), preferred_element_type=F32)

    def segment(z, carry, mask):
        soft = jnp.log(1.0 + jnp.exp(-jnp.abs(z)))
        log_beta = jnp.minimum(z, 0.0) - soft
        log_not = log_beta - z
        if mask is not None:
            log_not = jnp.where(mask, log_not, 0.0)
        n = z.shape[1]
        log_w = log_beta + _dot(log_not.astype(BF16), later[:n, :n])
        if carry is not None:
            log_w = log_w + carry
        a = jnp.exp(log_w)
        if mask is not None:
            a = jnp.where(mask, a, 0.0)
        return a.astype(BF16), jnp.sum(log_not, axis=-1, keepdims=True)

    def window(ref, start, size):
        return [ref[0, pl.ds(start, size), pair_lanes(p)] for p in range(ATTN_PAIRS)]

    def finish():
        for p in range(ATTN_PAIRS):
            base = p * HEADS_PER_PAIR
            o_ref[0, :, pair_lanes(p)] = jnp.where(
                head_of_lane == 0, acc_ref[base], acc_ref[base + 1]).astype(o_ref.dtype)

    @pl.when(qi == 0)
    def _first_block():
        keys = window(k_ref, 0, t)
        vals = window(v_ref, 0, t)
        for i, (p, _) in enumerate(heads):
            a, _ = segment(logits(q_heads[i], keys[p]), None, strict)
            acc_ref[i] = _dot(a, vals[p])
        finish()

    @pl.when(qi > 0)
    def _later_blocks():
        start = pl.multiple_of((qi - 1) * t, t)
        keys = window(k_ref, start, 2 * t)
        vals = window(v_ref, start, 2 * t)
        triangle = jnp.concatenate([strict[:half, :half]] * 2, axis=0)
        no_weights = jnp.zeros((half, half), BF16)
        z = [logits(q_heads[i], keys[p]) for i, (p, _) in enumerate(heads)]
        near = [segment(jnp.concatenate([zi[:half, t:t + half], zi[half:, t + half:]], axis=0),
                        None, triangle) for zi in z]
        far = [segment(jnp.concatenate([zi[:half, :t], zi[half:, half:t + half]], axis=0),
                       n[1], None) for zi, n in zip(z, near)]
        worst = []
        for i, (p, _) in enumerate(heads):
            upper = jnp.concatenate([far[i][0][:half], near[i][0][:half], no_weights], axis=1)
            lower = jnp.concatenate([no_weights, far[i][0][half:], near[i][0][half:]], axis=1)
            acc_ref[i] = _dot(jnp.concatenate([upper, lower], axis=0), vals[p])
            carry = near[i][1] + far[i][1]
            carry_ref[i] = carry
            worst.append(jnp.max(carry))

        def walk(i, rows, key_start, n_keys):
            lanes = pair_lanes(heads[i][0])
            keys = k_ref[0, pl.ds(key_start, n_keys), lanes]
            vals = v_ref[0, pl.ds(key_start, n_keys), lanes]
            carry = carry_ref[i, rows, :]
            a, total = segment(logits(q_heads[i][rows], keys), carry, None)
            acc_ref[i, rows, :] += _dot(a, vals)
            carry_ref[i, rows, :] = carry + total
            return jnp.max(carry_ref[i])

        def walk_where_needed(state, rows, key_start, n_keys):
            return tuple(lax.cond(w > LOG_WEIGHT_FLOOR,
                                  functools.partial(walk, i, rows, key_start, n_keys),
                                  lambda w=w: w)
                         for i, w in enumerate(state))

        def cond(state):
            j, worst = state[0], functools.reduce(jnp.maximum, state[1:])
            return jnp.logical_and(j >= 0, worst > LOG_WEIGHT_FLOOR)

        def body(state):
            j = state[0]
            return (j - 1,) + walk_where_needed(state[1:], slice(None), pl.multiple_of(j * t, t), t)

        state = walk_where_needed(tuple(worst), slice(half, t), start, half)
        lax.while_loop(cond, body, (qi - 2,) + state)
        finish()


def _attention(q, k, v):
    b, s, _ = q.shape
    t = ATTN_BLOCK
    width = ATTN_PAIRS * LANES
    n_heads = ATTN_PAIRS * HEADS_PER_PAIR
    return pl.pallas_call(
        _attn_kernel,
        grid=(b, SB_WIDTH // width, s // t),
        in_specs=[pl.BlockSpec((1, t, width), lambda bi, hi, qi: (bi, qi, hi)),
                  pl.BlockSpec((1, s, width), lambda bi, hi, qi: (bi, 0, hi)),
                  pl.BlockSpec((1, s, width), lambda bi, hi, qi: (bi, 0, hi))],
        out_specs=pl.BlockSpec((1, t, width), lambda bi, hi, qi: (bi, qi, hi)),
        out_shape=jax.ShapeDtypeStruct((b, s, SB_WIDTH), BF16),
        scratch_shapes=[pltpu.VMEM((n_heads, t, LANES), F32),
                        pltpu.VMEM((n_heads, t, 1), F32)],
        compiler_params=_params(3),
        name="attention",
    )(q, k, v)


def _merge_kernel(x_ref, osb_ref, p_ref, halo_ref, gu_ref, gv_ref, gate_ref, wpool_ref,
                  pscale_ref, ws_ref, bias_ref, wsb_ref, wpl_ref, wgm_ref, wout_ref, g_ref,
                  o_ref, *, tiles_per_seq, d_model):
    tm = x_ref.shape[0]
    tile_in_seq = pl.program_id(0) % tiles_per_seq
    group = lax.broadcasted_iota(jnp.int32, (1, POOL_WIDTH), 1) // POOL_GROUP_DIM

    halo = jnp.where(tile_in_seq == 0, 0.0, halo_ref[...])
    p = p_ref[...]
    ext = jnp.concatenate([halo, p], axis=0)
    sums = []
    s = ext
    for shift in (1, 2, 4, 8):
        s = s + pltpu.roll(s, shift, 0)
        sums.append(s[POOL_HALO:])
    window_sum = sums[-1]
    for g in range(len(POOL_WINDOWS) - 2, -1, -1):
        window_sum = jnp.where(group == g, sums[g], window_sum)
    pos = (tile_in_seq * tm + lax.broadcasted_iota(jnp.int32, (tm, 1), 0)).astype(F32)
    window = (2 << group).astype(F32)
    count = jnp.minimum(pos + 1.0, window)
    pooled = (window_sum / count - p).astype(BF16)
    pool_feat = (_dot(pooled, wpool_ref[...]) * pscale_ref[...]).astype(BF16)

    r = lax.broadcasted_iota(jnp.int32, (GM_CHUNK, GM_CHUNK), 0)
    c = lax.broadcasted_iota(jnp.int32, (GM_CHUNK, GM_CHUNK), 1)
    ws = [jnp.where(c <= r, ws_ref[g], 0.0).astype(BF16) for g in range(GM_GROUPS)]
    bias = bias_ref[...]
    chunks = []
    for ci in range(tm // GM_CHUNK):
        rows = slice(ci * GM_CHUNK, (ci + 1) * GM_CHUNK)
        vc = gv_ref[rows, :]
        mixed = _dot(ws[GM_GROUPS - 1], vc)
        for g in range(GM_GROUPS - 2, -1, -1):
            mixed = jnp.where(group == g, _dot(ws[g], vc), mixed)
        chunks.append((gu_ref[rows, :].astype(F32) * (mixed + bias)).astype(BF16))
    gm_feat = jnp.concatenate(chunks, axis=0)

    def gate(b):
        return gate_ref[:, b * d_model:(b + 1) * d_model].astype(F32)

    merged = (gate(0) * _dot(osb_ref[...], wsb_ref[...])
              + gate(1) * _dot(pool_feat, wpl_ref[...])
              + gate(2) * _dot(gm_feat, wgm_ref[...]))
    y = _dot(merged.astype(BF16), wout_ref[...])
    o_ref[...] = x_ref[...] + _rms(y, g_ref[...])


def _merge(x, o_sb, p_in, gu, gv, gates, w_pool_bd, pool_scale, w_spatial, bias, w_br_sb, w_br_pool,
           w_br_gm, w_out, g_post, seq):
    n, d = x.shape
    tm = TOKEN_TILE
    halo_blocks = tm // POOL_HALO

    def rows(width):
        return pl.BlockSpec((tm, width), lambda i: (i, 0))

    return pl.pallas_call(
        functools.partial(_merge_kernel, tiles_per_seq=seq // tm, d_model=d),
        grid=(n // tm,),
        in_specs=[rows(d), rows(SB_WIDTH), rows(POOL_WIDTH),
                  pl.BlockSpec((POOL_HALO, POOL_WIDTH),
                               lambda i: (jnp.maximum(i * halo_blocks - 1, 0), 0)),
                  rows(GM_WIDTH), rows(GM_WIDTH), rows(N_BRANCH * d),
                  _resident((POOL_WIDTH, POOL_WIDTH)), _resident((1, POOL_WIDTH)),
                  _resident((GM_GROUPS, GM_CHUNK, GM_CHUNK)), _resident((GM_CHUNK, GM_WIDTH)),
                  _resident((SB_WIDTH, d)), _resident((POOL_WIDTH, d)), _resident((GM_WIDTH, d)),
                  _resident((d, d)), _resident((1, d))],
        out_specs=rows(d),
        out_shape=jax.ShapeDtypeStruct((n, d), F32),
        compiler_params=_params(1),
        name="merge",
    )(x, o_sb, p_in, p_in, gu, gv, gates, w_pool_bd, pool_scale, w_spatial, bias, w_br_sb,
      w_br_pool, w_br_gm, w_out, g_post)


def _ffn_kernel(x_ref, gpre_ref, w1_ref, w2_ref, gpost_ref, o_ref):
    d_ff = w1_ref.shape[1]
    subs = [slice(r, r + SUB_TILE) for r in range(0, x_ref.shape[0], SUB_TILE)]
    h = [_rms(x_ref[rows, :], gpre_ref[...]).astype(BF16) for rows in subs]
    ff = [jnp.zeros((SUB_TILE, x_ref.shape[1]), F32) for _ in subs]
    for c in range(d_ff // FF_CHUNK):
        cols = slice(c * FF_CHUNK, (c + 1) * FF_CHUNK)
        for i in range(len(subs)):
            u = jnp.maximum(_dot(h[i], w1_ref[:, cols]), 0.0)
            ff[i] = ff[i] + _dot((u * u).astype(BF16), w2_ref[cols, :])
    for i, rows in enumerate(subs):
        o_ref[rows, :] = x_ref[rows, :] + _rms(ff[i], gpost_ref[...])


def _ffn(x, g_pre, w1, w2, g_post):
    n, d = x.shape
    tm = TOKEN_TILE
    d_ff = w1.shape[1]
    return pl.pallas_call(
        _ffn_kernel,
        grid=(n // tm,),
        in_specs=[pl.BlockSpec((tm, d), lambda i: (i, 0)), _resident((1, d)),
                  _resident((d, d_ff)), _resident((d_ff, d)), _resident((1, d))],
        out_specs=pl.BlockSpec((tm, d), lambda i: (i, 0)),
        out_shape=jax.ShapeDtypeStruct((n, d), F32),
        compiler_params=_params(1),
        name="ffn",
    )(x, g_pre, w1, w2, g_post)


def _block_diag(w):
    g, c, _ = w.shape
    eye = jnp.eye(g, dtype=w.dtype)
    return (eye[:, None, :, None] * w[:, :, None, :]).reshape(g * c, g * c)


def kernel(x, w_in, w_pool, pool_scale, gm_gain, w_spatial, b_spatial, w_br_sb, w_br_pool, w_br_gm,
           w_out, g_mix_pre, g_mix_post, g_ff_pre, g_ff_post, w_ff_in, w_ff_out):
    b, s, d = x.shape
    depth = w_in.shape[0]
    assert s % TOKEN_TILE == 0 and s % ATTN_BLOCK == 0
    assert TOKEN_TILE % GM_CHUNK == 0 and TOKEN_TILE % SUB_TILE == 0
    n = b * s
    xt = x.reshape(n, d)
    for l in range(depth):
        q, k, v, p_in, gu, gv, gates = _inproj(
            xt, g_mix_pre[l][None], gm_gain[l][None], w_in[l].astype(BF16))
        o_sb = _attention(q.reshape(b, s, SB_WIDTH), k.reshape(b, s, SB_WIDTH),
                          v.reshape(b, s, SB_WIDTH)).reshape(n, SB_WIDTH)
        bias = jnp.repeat(b_spatial[l].T, GM_GROUP_DIM, axis=1)
        xt = _merge(xt, o_sb, p_in, gu, gv, gates, _block_diag(w_pool[l]).astype(BF16),
                    pool_scale[l][None], w_spatial[l], bias, w_br_sb[l].astype(BF16),
                    w_br_pool[l].astype(BF16), w_br_gm[l].astype(BF16), w_out[l].astype(BF16),
                    g_mix_post[l][None], s)
        xt = _ffn(xt, g_ff_pre[l][None], w_ff_in[l].astype(BF16), w_ff_out[l].astype(BF16),
                  g_ff_post[l][None])
    return xt.reshape(b, s, d)
```

```python
import functools

import jax
import jax.numpy as jnp
from jax import lax
from jax.experimental import pallas as pl
from jax.experimental.pallas import tpu as pltpu

RMS_EPS = 1e-6
SB_HEADS = 8
SB_HEAD_DIM = 64
SB_WIDTH = SB_HEADS * SB_HEAD_DIM
POOL_WINDOWS = (2, 4, 8, 16)
POOL_GROUP_DIM = 64
POOL_WIDTH = len(POOL_WINDOWS) * POOL_GROUP_DIM
POOL_HALO = 16
GM_GROUPS = 4
GM_GROUP_DIM = 64
GM_WIDTH = GM_GROUPS * GM_GROUP_DIM
GM_CHUNK = 128
N_BRANCH = 3

LANES = 128
HEADS_PER_PAIR = LANES // SB_HEAD_DIM
ATTN_PAIRS = 4
ATTN_BLOCK = 256
ATTN_STEP_BLOCKS = 2
TOKEN_TILE = 1024
SUB_TILE = 512
FF_CHUNK = 512
MERGE_SLABS = 4
VMEM_LIMIT = 56 * 1024 * 1024

LOG_WEIGHT_FLOOR = -104.0

F32 = jnp.float32
BF16 = jnp.bfloat16


def _rms(x, gain):
    return x * lax.rsqrt(jnp.mean(x * x, axis=-1, keepdims=True) + RMS_EPS) * gain


def _dot(a, b):
    return jnp.dot(a, b, preferred_element_type=F32)


def _resident(shape):
    return pl.BlockSpec(shape, lambda *_: (0,) * len(shape), pipeline_mode=pl.Buffered(1))


def _params(n_axes):
    return pltpu.CompilerParams(dimension_semantics=("parallel",) * n_axes,
                                vmem_limit_bytes=VMEM_LIMIT)


def _inproj_kernel(x_ref, g_ref, gmg_ref, w_ref, q_ref, k_ref, v_ref, p_ref, gu_ref, gv_ref,
                   gate_ref, *, d_model):
    h = _rms(x_ref[...], g_ref[...]).astype(BF16)

    def proj(lo, width):
        return _dot(h, w_ref[:, lo:lo + width])

    q_at, k_at, v_at = 0, SB_WIDTH, 2 * SB_WIDTH
    p_at = 3 * SB_WIDTH
    gu_at = p_at + POOL_WIDTH
    gv_at = gu_at + GM_WIDTH
    gate_at = gv_at + GM_WIDTH
    for b in range(N_BRANCH):
        gate_ref[:, b * d_model:(b + 1) * d_model] = jax.nn.sigmoid(
            proj(gate_at + b * d_model, d_model)).astype(BF16)
    gu_ref[...] = jax.nn.gelu(proj(gu_at, GM_WIDTH)).astype(BF16)
    gv_ref[...] = _rms(jax.nn.gelu(proj(gv_at, GM_WIDTH)), gmg_ref[...]).astype(BF16)
    p_ref[...] = proj(p_at, POOL_WIDTH)
    q_ref[...] = (proj(q_at, SB_WIDTH) * (SB_HEAD_DIM ** -0.5)).astype(BF16)
    k_ref[...] = proj(k_at, SB_WIDTH).astype(BF16)
    v_ref[...] = proj(v_at, SB_WIDTH).astype(BF16)


def _inproj(x, g_pre, gm_gain, w_in):
    n, d = x.shape
    tm = TOKEN_TILE
    d_in = w_in.shape[1]

    def rows(width, dtype):
        return pl.BlockSpec((tm, width), lambda i: (i, 0)), jax.ShapeDtypeStruct((n, width), dtype)

    outs = [rows(SB_WIDTH, BF16), rows(SB_WIDTH, BF16), rows(SB_WIDTH, BF16), rows(POOL_WIDTH, F32),
            rows(GM_WIDTH, BF16), rows(GM_WIDTH, BF16), rows(N_BRANCH * d, BF16)]
    return pl.pallas_call(
        functools.partial(_inproj_kernel, d_model=d),
        grid=(n // tm,),
        in_specs=[pl.BlockSpec((tm, d), lambda i: (i, 0)), _resident((1, d)),
                  _resident((1, GM_WIDTH)), _resident((d, d_in))],
        out_specs=[s for s, _ in outs],
        out_shape=[s for _, s in outs],
        compiler_params=_params(1),
        name="inproj",
    )(x, g_pre, gm_gain, w_in)


def _attn_kernel(q_ref, k_ref, v_ref, o_ref, acc_ref, carry_ref):
    t = ATTN_BLOCK
    half = t // 2
    step = pl.program_id(2)
    row = lax.broadcasted_iota(jnp.int32, (t, t), 0)
    col = lax.broadcasted_iota(jnp.int32, (t, t), 1)
    strict = col < row
    later = (row > col).astype(BF16)
    head_of_lane = lax.broadcasted_iota(jnp.int32, (1, LANES), 1) // SB_HEAD_DIM
    heads = [(p, h) for p in range(ATTN_PAIRS) for h in range(HEADS_PER_PAIR)]
    n_heads = len(heads)

    def pair_lanes(p):
        return slice(p * LANES, (p + 1) * LANES)

    def block_rows(b):
        return slice(b * t, (b + 1) * t)

    def head_queries(b, i):
        p, h = heads[i]
        q = q_ref[0, block_rows(b), pair_lanes(p)]
        return jnp.where(head_of_lane == h, q, jnp.zeros_like(q))

    q_heads = [[head_queries(b, i) for i in range(n_heads)] for b in range(ATTN_STEP_BLOCKS)]

    def logits(qh, keys):
        return lax.dot_general(qh, keys, (((1,), (1,)), ((), ())), preferred_element_type=F32)

    def segment(z, carry, mask):
        soft = jnp.log(1.0 + jnp.exp(-jnp.abs(z)))
        log_beta = jnp.minimum(z, 0.0) - soft
        log_not = log_beta - z
        if mask is not None:
            log_not = jnp.where(mask, log_not, 0.0)
        n = z.shape[1]
        log_w = log_beta + _dot(log_not.astype(BF16), later[:n, :n])
        if carry is not None:
            log_w = log_w + carry
        a = jnp.exp(log_w)
        if mask is not None:
            a = jnp.where(mask, a, 0.0)
        return a.astype(BF16), jnp.sum(log_not, axis=-1, keepdims=True)

    def window(ref, start, size):
        return [ref[0, pl.ds(start, size), pair_lanes(p)] for p in range(ATTN_PAIRS)]

    def finish(b):
        for p in range(ATTN_PAIRS):
            slot = b * n_heads + p * HEADS_PER_PAIR
            o_ref[0, block_rows(b), pair_lanes(p)] = jnp.where(
                head_of_lane == 0, acc_ref[slot], acc_ref[slot + 1]).astype(o_ref.dtype)

    def sequence_start(b):
        keys = window(k_ref, 0, t)
        vals = window(v_ref, 0, t)
        for i, (p, _) in enumerate(heads):
            a, _ = segment(logits(q_heads[b][i], keys[p]), None, strict)
            acc_ref[b * n_heads + i] = _dot(a, vals[p])

    def windows(blocks, keys, vals):
        triangle = jnp.concatenate([strict[:half, :half]] * 2, axis=0)
        no_weights = jnp.zeros((half, half), BF16)
        units = [(b, i) for b in blocks for i in range(n_heads)]
        z = [logits(q_heads[b][i], keys[b][heads[i][0]]) for b, i in units]
        near = [segment(jnp.concatenate([zu[:half, t:t + half], zu[half:, t + half:]], axis=0),
                        None, triangle) for zu in z]
        far = [segment(jnp.concatenate([zu[:half, :t], zu[half:, half:t + half]], axis=0),
                       n[1], None) for zu, n in zip(z, near)]
        worst = {b: [] for b in blocks}
        for (b, i), n, f in zip(units, near, far):
            upper = jnp.concatenate([f[0][:half], n[0][:half], no_weights], axis=1)
            lower = jnp.concatenate([no_weights, f[0][half:], n[0][half:]], axis=1)
            slot = b * n_heads + i
            acc_ref[slot] = _dot(jnp.concatenate([upper, lower], axis=0), vals[b][heads[i][0]])
            carry = n[1] + f[1]
            carry_ref[slot] = carry
            worst[b].append(jnp.max(carry))
        return worst

    def walk(b, i, rows, key_start, n_keys):
        lanes = pair_lanes(heads[i][0])
        slot = b * n_heads + i
        keys = k_ref[0, pl.ds(key_start, n_keys), lanes]
        vals = v_ref[0, pl.ds(key_start, n_keys), lanes]
        carry = carry_ref[slot, rows, :]
        a, total = segment(logits(q_heads[b][i][rows], keys), carry, None)
        acc_ref[slot, rows, :] += _dot(a, vals)
        carry_ref[slot, rows, :] = carry + total
        return jnp.max(carry_ref[slot])

    def walk_where_needed(b, state, rows, key_start, n_keys):
        return tuple(lax.cond(w > LOG_WEIGHT_FLOOR,
                              functools.partial(walk, b, i, rows, key_start, n_keys),
                              lambda w=w: w)
                     for i, w in enumerate(state))

    def walk_left(b, block_index, worst):
        window_start = pl.multiple_of((block_index - 1) * t, t)
        state = walk_where_needed(b, tuple(worst), slice(half, t), window_start, half)

        def cond(state):
            j, worst = state[0], functools.reduce(jnp.maximum, state[1:])
            return jnp.logical_and(j >= 0, worst > LOG_WEIGHT_FLOOR)

        def body(state):
            j = state[0]
            return (j - 1,) + walk_where_needed(b, state[1:], slice(None),
                                                pl.multiple_of(j * t, t), t)

        lax.while_loop(cond, body, (block_index - 2,) + state)

    first_index = step * ATTN_STEP_BLOCKS
    all_blocks = list(range(ATTN_STEP_BLOCKS))

    @pl.when(step == 0)
    def _first_step():
        sequence_start(0)
        later_blocks = all_blocks[1:]
        keys = {b: window(k_ref, (b - 1) * t, 2 * t) for b in later_blocks}
        vals = {b: window(v_ref, (b - 1) * t, 2 * t) for b in later_blocks}
        worst = windows(later_blocks, keys, vals)
        for b in later_blocks:
            walk_left(b, b, worst[b])
        for b in all_blocks:
            finish(b)

    @pl.when(step > 0)
    def _later_steps():
        span = (ATTN_STEP_BLOCKS + 1) * t
        start = pl.multiple_of((first_index - 1) * t, t)
        keys_span = window(k_ref, start, span)
        vals_span = window(v_ref, start, span)
        keys = {b: [kp[b * t:(b + 2) * t] for kp in keys_span] for b in all_blocks}
        vals = {b: [vp[b * t:(b + 2) * t] for vp in vals_span] for b in all_blocks}
        worst = windows(all_blocks, keys, vals)
        for b in all_blocks:
            walk_left(b, first_index + b, worst[b])
        for b in all_blocks:
            finish(b)


def _attention(q, k, v):
    b, s, _ = q.shape
    rows = ATTN_STEP_BLOCKS * ATTN_BLOCK
    width = ATTN_PAIRS * LANES
    slots = ATTN_STEP_BLOCKS * ATTN_PAIRS * HEADS_PER_PAIR
    return pl.pallas_call(
        _attn_kernel,
        grid=(b, SB_WIDTH // width, s // rows),
        in_specs=[pl.BlockSpec((1, rows, width), lambda bi, hi, qi: (bi, qi, hi)),
                  pl.BlockSpec((1, s, width), lambda bi, hi, qi: (bi, 0, hi)),
                  pl.BlockSpec((1, s, width), lambda bi, hi, qi: (bi, 0, hi))],
        out_specs=pl.BlockSpec((1, rows, width), lambda bi, hi, qi: (bi, qi, hi)),
        out_shape=jax.ShapeDtypeStruct((b, s, SB_WIDTH), BF16),
        scratch_shapes=[pltpu.VMEM((slots, ATTN_BLOCK, LANES), F32),
                        pltpu.VMEM((slots, ATTN_BLOCK, 1), F32)],
        compiler_params=_params(3),
        name="attention",
    )(q, k, v)


def _merge_kernel(x_ref, osb_ref, p_ref, halo_ref, gu_ref, gv_ref, gate_ref, wpool_ref,
                  pscale_ref, ws_ref, bias_ref, wsb_ref, wpl_ref, wgm_ref, wout_ref, g_ref,
                  o_ref, *, tiles_per_seq, d_model):
    tm = x_ref.shape[0]
    tile_in_seq = pl.program_id(0) % tiles_per_seq
    group = lax.broadcasted_iota(jnp.int32, (1, POOL_WIDTH), 1) // POOL_GROUP_DIM

    def gate(b):
        return gate_ref[:, b * d_model:(b + 1) * d_model].astype(F32)

    window = (2 << group).astype(F32)

    def pooled_rows(r0, n_rows):
        if r0 == 0:
            halo = jnp.where(tile_in_seq == 0, 0.0, halo_ref[...])
        else:
            halo = p_ref[r0 - POOL_HALO:r0, :]
        p = p_ref[r0:r0 + n_rows, :]
        s = jnp.concatenate([halo, p], axis=0)
        sums = []
        for shift in (1, 2, 4, 8):
            s = s + pltpu.roll(s, shift, 0)
            sums.append(s[POOL_HALO:])
        window_sum = sums[-1]
        for g in range(len(POOL_WINDOWS) - 2, -1, -1):
            window_sum = jnp.where(group == g, sums[g], window_sum)
        pos = tile_in_seq * tm + r0 + lax.broadcasted_iota(jnp.int32, (n_rows, 1), 0)
        count = jnp.minimum(pos.astype(F32) + 1.0, window)
        return (window_sum / count - p).astype(BF16)

    slab = d_model // MERGE_SLABS
    chunk = tm // MERGE_SLABS
    sb_slabs, pool_chunks = [], []
    for j in range(MERGE_SLABS):
        sb_slabs.append(_dot(osb_ref[...], wsb_ref[:, j * slab:(j + 1) * slab]))
        pooled = pooled_rows(j * chunk, chunk)
        pool_feat = (_dot(pooled, wpool_ref[...]) * pscale_ref[...]).astype(BF16)
        pool_chunks.append(_dot(pool_feat, wpl_ref[...]))
    sb_part = gate(0) * jnp.concatenate(sb_slabs, axis=1)
    pool_part = gate(1) * jnp.concatenate(pool_chunks, axis=0)

    r = lax.broadcasted_iota(jnp.int32, (GM_CHUNK, GM_CHUNK), 0)
    c = lax.broadcasted_iota(jnp.int32, (GM_CHUNK, GM_CHUNK), 1)
    ws = [jnp.where(c <= r, ws_ref[g], 0.0).astype(BF16) for g in range(GM_GROUPS)]
    bias = bias_ref[...]
    chunks = []
    for ci in range(tm // GM_CHUNK):
        rows = slice(ci * GM_CHUNK, (ci + 1) * GM_CHUNK)
        vc = gv_ref[rows, :]
        mixed = _dot(ws[GM_GROUPS - 1], vc)
        for g in range(GM_GROUPS - 2, -1, -1):
            mixed = jnp.where(group == g, _dot(ws[g], vc), mixed)
        chunks.append((gu_ref[rows, :].astype(F32) * (mixed + bias)).astype(BF16))
    gm_feat = jnp.concatenate(chunks, axis=0)

    merged = (sb_part + pool_part + gate(2) * _dot(gm_feat, wgm_ref[...])).astype(BF16)
    for r0 in range(0, tm, SUB_TILE):
        rows = slice(r0, r0 + SUB_TILE)
        y = _dot(merged[rows], wout_ref[...])
        o_ref[rows, :] = x_ref[rows, :] + _rms(y, g_ref[...])


def _merge(x, o_sb, p_in, gu, gv, gates, w_pool_bd, pool_scale, w_spatial, bias, w_br_sb, w_br_pool,
           w_br_gm, w_out, g_post, seq):
    n, d = x.shape
    tm = TOKEN_TILE
    halo_blocks = tm // POOL_HALO

    def rows(width):
        return pl.BlockSpec((tm, width), lambda i: (i, 0))

    return pl.pallas_call(
        functools.partial(_merge_kernel, tiles_per_seq=seq // tm, d_model=d),
        grid=(n // tm,),
        in_specs=[rows(d), rows(SB_WIDTH), rows(POOL_WIDTH),
                  pl.BlockSpec((POOL_HALO, POOL_WIDTH),
                               lambda i: (jnp.maximum(i * halo_blocks - 1, 0), 0)),
                  rows(GM_WIDTH), rows(GM_WIDTH), rows(N_BRANCH * d),
                  _resident((POOL_WIDTH, POOL_WIDTH)), _resident((1, POOL_WIDTH)),
                  _resident((GM_GROUPS, GM_CHUNK, GM_CHUNK)), _resident((GM_CHUNK, GM_WIDTH)),
                  _resident((SB_WIDTH, d)), _resident((POOL_WIDTH, d)), _resident((GM_WIDTH, d)),
                  _resident((d, d)), _resident((1, d))],
        out_specs=rows(d),
        out_shape=jax.ShapeDtypeStruct((n, d), F32),
        compiler_params=_params(1),
        name="merge",
    )(x, o_sb, p_in, p_in, gu, gv, gates, w_pool_bd, pool_scale, w_spatial, bias, w_br_sb,
      w_br_pool, w_br_gm, w_out, g_post)


def _ffn_kernel(x_ref, gpre_ref, w1_ref, w2_ref, gpost_ref, o_ref):
    d_ff = w1_ref.shape[1]
    subs = [slice(r, r + SUB_TILE) for r in range(0, x_ref.shape[0], SUB_TILE)]
    h = [_rms(x_ref[rows, :], gpre_ref[...]).astype(BF16) for rows in subs]
    ff = [jnp.zeros((SUB_TILE, x_ref.shape[1]), F32) for _ in subs]
    for c in range(d_ff // FF_CHUNK):
        cols = slice(c * FF_CHUNK, (c + 1) * FF_CHUNK)
        for i in range(len(subs)):
            u = jnp.maximum(_dot(h[i], w1_ref[:, cols]), 0.0)
            ff[i] = ff[i] + _dot((u * u).astype(BF16), w2_ref[cols, :])
    for i, rows in enumerate(subs):
        o_ref[rows, :] = x_ref[rows, :] + _rms(ff[i], gpost_ref[...])


def _ffn(x, g_pre, w1, w2, g_post):
    n, d = x.shape
    tm = TOKEN_TILE
    d_ff = w1.shape[1]
    return pl.pallas_call(
        _ffn_kernel,
        grid=(n // tm,),
        in_specs=[pl.BlockSpec((tm, d), lambda i: (i, 0)), _resident((1, d)),
                  _resident((d, d_ff)), _resident((d_ff, d)), _resident((1, d))],
        out_specs=pl.BlockSpec((tm, d), lambda i: (i, 0)),
        out_shape=jax.ShapeDtypeStruct((n, d), F32),
        compiler_params=_params(1),
        name="ffn",
    )(x, g_pre, w1, w2, g_post)


def _block_diag(w):
    g, c, _ = w.shape
    eye = jnp.eye(g, dtype=w.dtype)
    return (eye[:, None, :, None] * w[:, :, None, :]).reshape(g * c, g * c)


def kernel(x, w_in, w_pool, pool_scale, gm_gain, w_spatial, b_spatial, w_br_sb, w_br_pool, w_br_gm,
           w_out, g_mix_pre, g_mix_post, g_ff_pre, g_ff_post, w_ff_in, w_ff_out):
    b, s, d = x.shape
    depth = w_in.shape[0]
    assert s % TOKEN_TILE == 0 and s % (ATTN_STEP_BLOCKS * ATTN_BLOCK) == 0
    assert TOKEN_TILE % GM_CHUNK == 0 and TOKEN_TILE % SUB_TILE == 0
    n = b * s
    xt = x.reshape(n, d)
    for l in range(depth):
        q, k, v, p_in, gu, gv, gates = _inproj(
            xt, g_mix_pre[l][None], gm_gain[l][None], w_in[l].astype(BF16))
        o_sb = _attention(q.reshape(b, s, SB_WIDTH), k.reshape(b, s, SB_WIDTH),
                          v.reshape(b, s, SB_WIDTH)).reshape(n, SB_WIDTH)
        bias = jnp.repeat(b_spatial[l].T, GM_GROUP_DIM, axis=1)
        xt = _merge(xt, o_sb, p_in, gu, gv, gates, _block_diag(w_pool[l]).astype(BF16),
                    pool_scale[l][None], w_spatial[l], bias, w_br_sb[l].astype(BF16),
                    w_br_pool[l].astype(BF16), w_br_gm[l].astype(BF16), w_out[l].astype(BF16),
                    g_mix_post[l][None], s)
        xt = _ffn(xt, g_ff_pre[l][None], w_ff_in[l].astype(BF16), w_ff_out[l].astype(BF16),
                  g_ff_post[l][None])
    return xt.reshape(b, s, d)
```

```python
import functools

import jax
import jax.numpy as jnp
from jax import lax
from jax.experimental import pallas as pl
from jax.experimental.pallas import tpu as pltpu

RMS_EPS = 1e-6
SB_HEADS = 8
SB_HEAD_DIM = 64
SB_WIDTH = SB_HEADS * SB_HEAD_DIM
POOL_WINDOWS = (2, 4, 8, 16)
POOL_GROUP_DIM = 64
POOL_WIDTH = len(POOL_WINDOWS) * POOL_GROUP_DIM
POOL_HALO = 16
GM_GROUPS = 4
GM_GROUP_DIM = 64
GM_WIDTH = GM_GROUPS * GM_GROUP_DIM
GM_CHUNK = 128
N_BRANCH = 3

LANES = 128
HEADS_PER_PAIR = LANES // SB_HEAD_DIM
ATTN_PAIRS = 4
ATTN_BLOCK = 256
ATTN_STEP_BLOCKS = 2
TOKEN_TILE = 1024
SUB_TILE = 512
FF_CHUNK = 512
MERGE_SLABS = 4
VMEM_LIMIT = 56 * 1024 * 1024

LOG_WEIGHT_FLOOR = -104.0

F32 = jnp.float32
BF16 = jnp.bfloat16


def _rms(x, gain):
    return x * lax.rsqrt(jnp.mean(x * x, axis=-1, keepdims=True) + RMS_EPS) * gain


def _dot(a, b):
    return jnp.dot(a, b, preferred_element_type=F32)


def _resident(shape):
    return pl.BlockSpec(shape, lambda *_: (0,) * len(shape), pipeline_mode=pl.Buffered(1))


def _layer_resident(layer, shape):
    return pl.BlockSpec((None,) + shape, lambda *_: (layer,) + (0,) * len(shape),
                        pipeline_mode=pl.Buffered(1))


def _params(n_axes):
    return pltpu.CompilerParams(dimension_semantics=("parallel",) * n_axes,
                                vmem_limit_bytes=VMEM_LIMIT)


def _inproj_kernel(x_ref, g_ref, gmg_ref, w_ref, q_ref, k_ref, v_ref, p_ref, gu_ref, gv_ref,
                   gate_ref, *, d_model):
    h = _rms(x_ref[...], g_ref[...]).astype(BF16)

    def proj(lo, width):
        return _dot(h, w_ref[:, lo:lo + width])

    q_at, k_at, v_at = 0, SB_WIDTH, 2 * SB_WIDTH
    p_at = 3 * SB_WIDTH
    gu_at = p_at + POOL_WIDTH
    gv_at = gu_at + GM_WIDTH
    gate_at = gv_at + GM_WIDTH
    for b in range(N_BRANCH):
        gate_ref[:, b * d_model:(b + 1) * d_model] = jax.nn.sigmoid(
            proj(gate_at + b * d_model, d_model)).astype(BF16)
    gu_ref[...] = jax.nn.gelu(proj(gu_at, GM_WIDTH)).astype(BF16)
    gv_ref[...] = _rms(jax.nn.gelu(proj(gv_at, GM_WIDTH)), gmg_ref[...]).astype(BF16)
    p_ref[...] = proj(p_at, POOL_WIDTH)
    q_ref[...] = (proj(q_at, SB_WIDTH) * (SB_HEAD_DIM ** -0.5)).astype(BF16)
    k_ref[...] = proj(k_at, SB_WIDTH).astype(BF16)
    v_ref[...] = proj(v_at, SB_WIDTH).astype(BF16)


def _inproj(layer, x, g_pre, gm_gain, w_in):
    n, d = x.shape
    tm = TOKEN_TILE
    d_in = w_in.shape[2]

    def rows(width, dtype):
        return pl.BlockSpec((tm, width), lambda i: (i, 0)), jax.ShapeDtypeStruct((n, width), dtype)

    outs = [rows(SB_WIDTH, BF16), rows(SB_WIDTH, BF16), rows(SB_WIDTH, BF16), rows(POOL_WIDTH, F32),
            rows(GM_WIDTH, BF16), rows(GM_WIDTH, BF16), rows(N_BRANCH * d, BF16)]
    return pl.pallas_call(
        functools.partial(_inproj_kernel, d_model=d),
        grid=(n // tm,),
        in_specs=[pl.BlockSpec((tm, d), lambda i: (i, 0)), _resident((1, d)),
                  _resident((1, GM_WIDTH)), _layer_resident(layer, (d, d_in))],
        out_specs=[s for s, _ in outs],
        out_shape=[s for _, s in outs],
        compiler_params=_params(1),
        name="inproj",
    )(x, g_pre, gm_gain, w_in)


def _attn_kernel(q_ref, k_ref, v_ref, o_ref, acc_ref, carry_ref):
    t = ATTN_BLOCK
    half = t // 2
    step = pl.program_id(2)
    row = lax.broadcasted_iota(jnp.int32, (t, t), 0)
    col = lax.broadcasted_iota(jnp.int32, (t, t), 1)
    strict = col < row
    later = (row > col).astype(BF16)
    head_of_lane = lax.broadcasted_iota(jnp.int32, (1, LANES), 1) // SB_HEAD_DIM
    heads = [(p, h) for p in range(ATTN_PAIRS) for h in range(HEADS_PER_PAIR)]
    n_heads = len(heads)

    def pair_lanes(p):
        return slice(p * LANES, (p + 1) * LANES)

    def block_rows(b):
        return slice(b * t, (b + 1) * t)

    def head_queries(b, i):
        p, h = heads[i]
        q = q_ref[0, block_rows(b), pair_lanes(p)]
        return jnp.where(head_of_lane == h, q, jnp.zeros_like(q))

    q_heads = [[head_queries(b, i) for i in range(n_heads)] for b in range(ATTN_STEP_BLOCKS)]

    def logits(qh, keys):
        return lax.dot_general(qh, keys, (((1,), (1,)), ((), ())), preferred_element_type=F32)

    def segment(z, carry, mask):
        soft = jnp.log(1.0 + jnp.exp(-jnp.abs(z)))
        log_beta = jnp.minimum(z, 0.0) - soft
        log_not = log_beta - z
        if mask is not None:
            log_not = jnp.where(mask, log_not, 0.0)
        n = z.shape[1]
        log_w = log_beta + _dot(log_not.astype(BF16), later[:n, :n])
        if carry is not None:
            log_w = log_w + carry
        a = jnp.exp(log_w)
        if mask is not None:
            a = jnp.where(mask, a, 0.0)
        return a.astype(BF16), jnp.sum(log_not, axis=-1, keepdims=True)

    def window(ref, start, size):
        return [ref[0, pl.ds(start, size), pair_lanes(p)] for p in range(ATTN_PAIRS)]

    def finish(b):
        for p in range(ATTN_PAIRS):
            slot = b * n_heads + p * HEADS_PER_PAIR
            o_ref[0, block_rows(b), pair_lanes(p)] = jnp.where(
                head_of_lane == 0, acc_ref[slot], acc_ref[slot + 1]).astype(o_ref.dtype)

    def sequence_start(b):
        keys = window(k_ref, 0, t)
        vals = window(v_ref, 0, t)
        for i, (p, _) in enumerate(heads):
            a, _ = segment(logits(q_heads[b][i], keys[p]), None, strict)
            acc_ref[b * n_heads + i] = _dot(a, vals[p])

    def windows(blocks, keys, vals):
        triangle = jnp.concatenate([strict[:half, :half]] * 2, axis=0)
        no_weights = jnp.zeros((half, half), BF16)
        units = [(b, i) for b in blocks for i in range(n_heads)]
        z = [logits(q_heads[b][i], keys[b][heads[i][0]]) for b, i in units]
        near = [segment(jnp.concatenate([zu[:half, t:t + half], zu[half:, t + half:]], axis=0),
                        None, triangle) for zu in z]
        far = [segment(jnp.concatenate([zu[:half, :t], zu[half:, half:t + half]], axis=0),
                       n[1], None) for zu, n in zip(z, near)]
        worst = {b: [] for b in blocks}
        for (b, i), n, f in zip(units, near, far):
            upper = jnp.concatenate([f[0][:half], n[0][:half], no_weights], axis=1)
            lower = jnp.concatenate([no_weights, f[0][half:], n[0][half:]], axis=1)
            slot = b * n_heads + i
            acc_ref[slot] = _dot(jnp.concatenate([upper, lower], axis=0), vals[b][heads[i][0]])
            carry = n[1] + f[1]
            carry_ref[slot] = carry
            worst[b].append(jnp.max(carry))
        return worst

    def walk(b, i, rows, key_start, n_keys):
        lanes = pair_lanes(heads[i][0])
        slot = b * n_heads + i
        keys = k_ref[0, pl.ds(key_start, n_keys), lanes]
        vals = v_ref[0, pl.ds(key_start, n_keys), lanes]
        carry = carry_ref[slot, rows, :]
        a, total = segment(logits(q_heads[b][i][rows], keys), carry, None)
        acc_ref[slot, rows, :] += _dot(a, vals)
        carry_ref[slot, rows, :] = carry + total
        return jnp.max(carry_ref[slot])

    def walk_where_needed(b, state, rows, key_start, n_keys):
        return tuple(lax.cond(w > LOG_WEIGHT_FLOOR,
                              functools.partial(walk, b, i, rows, key_start, n_keys),
                              lambda w=w: w)
                     for i, w in enumerate(state))

    def walk_left(b, block_index, worst):
        window_start = pl.multiple_of((block_index - 1) * t, t)
        state = walk_where_needed(b, tuple(worst), slice(half, t), window_start, half)

        def cond(state):
            j, worst = state[0], functools.reduce(jnp.maximum, state[1:])
            return jnp.logical_and(j >= 0, worst > LOG_WEIGHT_FLOOR)

        def body(state):
            j = state[0]
            return (j - 1,) + walk_where_needed(b, state[1:], slice(None),
                                                pl.multiple_of(j * t, t), t)

        lax.while_loop(cond, body, (block_index - 2,) + state)

    first_index = step * ATTN_STEP_BLOCKS
    all_blocks = list(range(ATTN_STEP_BLOCKS))

    @pl.when(step == 0)
    def _first_step():
        sequence_start(0)
        later_blocks = all_blocks[1:]
        keys = {b: window(k_ref, (b - 1) * t, 2 * t) for b in later_blocks}
        vals = {b: window(v_ref, (b - 1) * t, 2 * t) for b in later_blocks}
        worst = windows(later_blocks, keys, vals)
        for b in later_blocks:
            walk_left(b, b, worst[b])
        for b in all_blocks:
            finish(b)

    @pl.when(step > 0)
    def _later_steps():
        span = (ATTN_STEP_BLOCKS + 1) * t
        start = pl.multiple_of((first_index - 1) * t, t)
        keys_span = window(k_ref, start, span)
        vals_span = window(v_ref, start, span)
        keys = {b: [kp[b * t:(b + 2) * t] for kp in keys_span] for b in all_blocks}
        vals = {b: [vp[b * t:(b + 2) * t] for vp in vals_span] for b in all_blocks}
        worst = windows(all_blocks, keys, vals)
        for b in all_blocks:
            walk_left(b, first_index + b, worst[b])
        for b in all_blocks:
            finish(b)


def _attention(q, k, v):
    b, s, _ = q.shape
    rows = ATTN_STEP_BLOCKS * ATTN_BLOCK
    width = ATTN_PAIRS * LANES
    slots = ATTN_STEP_BLOCKS * ATTN_PAIRS * HEADS_PER_PAIR
    return pl.pallas_call(
        _attn_kernel,
        grid=(b, SB_WIDTH // width, s // rows),
        in_specs=[pl.BlockSpec((1, rows, width), lambda bi, hi, qi: (bi, qi, hi)),
                  pl.BlockSpec((1, s, width), lambda bi, hi, qi: (bi, 0, hi)),
                  pl.BlockSpec((1, s, width), lambda bi, hi, qi: (bi, 0, hi))],
        out_specs=pl.BlockSpec((1, rows, width), lambda bi, hi, qi: (bi, qi, hi)),
        out_shape=jax.ShapeDtypeStruct((b, s, SB_WIDTH), BF16),
        scratch_shapes=[pltpu.VMEM((slots, ATTN_BLOCK, LANES), F32),
                        pltpu.VMEM((slots, ATTN_BLOCK, 1), F32)],
        compiler_params=_params(3),
        name="attention",
    )(q, k, v)


def _merge_kernel(x_ref, osb_ref, p_ref, halo_ref, gu_ref, gv_ref, gate_ref, wpool_ref,
                  pscale_ref, ws_ref, bias_ref, wsb_ref, wpl_ref, wgm_ref, wout_ref, g_ref,
                  o_ref, *, tiles_per_seq, d_model):
    tm = x_ref.shape[0]
    tile_in_seq = pl.program_id(0) % tiles_per_seq
    group = lax.broadcasted_iota(jnp.int32, (1, POOL_WIDTH), 1) // POOL_GROUP_DIM

    def gate(b):
        return gate_ref[:, b * d_model:(b + 1) * d_model].astype(F32)

    window = (2 << group).astype(F32)

    def pooled_rows(r0, n_rows):
        if r0 == 0:
            halo = jnp.where(tile_in_seq == 0, 0.0, halo_ref[...])
        else:
            halo = p_ref[r0 - POOL_HALO:r0, :]
        p = p_ref[r0:r0 + n_rows, :]
        s = jnp.concatenate([halo, p], axis=0)
        sums = []
        for shift in (1, 2, 4, 8):
            s = s + pltpu.roll(s, shift, 0)
            sums.append(s[POOL_HALO:])
        window_sum = sums[-1]
        for g in range(len(POOL_WINDOWS) - 2, -1, -1):
            window_sum = jnp.where(group == g, sums[g], window_sum)
        pos = tile_in_seq * tm + r0 + lax.broadcasted_iota(jnp.int32, (n_rows, 1), 0)
        count = jnp.minimum(pos.astype(F32) + 1.0, window)
        return (window_sum / count - p).astype(BF16)

    slab = d_model // MERGE_SLABS
    chunk = tm // MERGE_SLABS
    sb_slabs, pool_chunks = [], []
    for j in range(MERGE_SLABS):
        sb_slabs.append(_dot(osb_ref[...], wsb_ref[:, j * slab:(j + 1) * slab]))
        pooled = pooled_rows(j * chunk, chunk)
        pool_feat = (_dot(pooled, wpool_ref[...]) * pscale_ref[...]).astype(BF16)
        pool_chunks.append(_dot(pool_feat, wpl_ref[...]))
    sb_part = gate(0) * jnp.concatenate(sb_slabs, axis=1)
    pool_part = gate(1) * jnp.concatenate(pool_chunks, axis=0)

    r = lax.broadcasted_iota(jnp.int32, (GM_CHUNK, GM_CHUNK), 0)
    c = lax.broadcasted_iota(jnp.int32, (GM_CHUNK, GM_CHUNK), 1)
    ws = [jnp.where(c <= r, ws_ref[g], 0.0).astype(BF16) for g in range(GM_GROUPS)]
    bias = bias_ref[...]
    chunks = []
    for ci in range(tm // GM_CHUNK):
        rows = slice(ci * GM_CHUNK, (ci + 1) * GM_CHUNK)
        vc = gv_ref[rows, :]
        mixed = _dot(ws[GM_GROUPS - 1], vc)
        for g in range(GM_GROUPS - 2, -1, -1):
            mixed = jnp.where(group == g, _dot(ws[g], vc), mixed)
        chunks.append((gu_ref[rows, :].astype(F32) * (mixed + bias)).astype(BF16))
    gm_feat = jnp.concatenate(chunks, axis=0)

    merged = (sb_part + pool_part + gate(2) * _dot(gm_feat, wgm_ref[...])).astype(BF16)
    for r0 in range(0, tm, SUB_TILE):
        rows = slice(r0, r0 + SUB_TILE)
        y = _dot(merged[rows], wout_ref[...])
        o_ref[rows, :] = x_ref[rows, :] + _rms(y, g_ref[...])


def _merge(layer, x, o_sb, p_in, gu, gv, gates, w_pool_bd, pool_scale, w_spatial, bias, w_br_sb,
           w_br_pool, w_br_gm, w_out, g_post, seq):
    n, d = x.shape
    tm = TOKEN_TILE
    halo_blocks = tm // POOL_HALO

    def rows(width):
        return pl.BlockSpec((tm, width), lambda i: (i, 0))

    return pl.pallas_call(
        functools.partial(_merge_kernel, tiles_per_seq=seq // tm, d_model=d),
        grid=(n // tm,),
        in_specs=[rows(d), rows(SB_WIDTH), rows(POOL_WIDTH),
                  pl.BlockSpec((POOL_HALO, POOL_WIDTH),
                               lambda i: (jnp.maximum(i * halo_blocks - 1, 0), 0)),
                  rows(GM_WIDTH), rows(GM_WIDTH), rows(N_BRANCH * d),
                  _resident((POOL_WIDTH, POOL_WIDTH)), _resident((1, POOL_WIDTH)),
                  _resident((GM_GROUPS, GM_CHUNK, GM_CHUNK)), _resident((GM_CHUNK, GM_WIDTH)),
                  _layer_resident(layer, (SB_WIDTH, d)), _layer_resident(layer, (POOL_WIDTH, d)),
                  _layer_resident(layer, (GM_WIDTH, d)), _layer_resident(layer, (d, d)),
                  _resident((1, d))],
        out_specs=rows(d),
        out_shape=jax.ShapeDtypeStruct((n, d), F32),
        compiler_params=_params(1),
        name="merge",
    )(x, o_sb, p_in, p_in, gu, gv, gates, w_pool_bd, pool_scale, w_spatial, bias, w_br_sb,
      w_br_pool, w_br_gm, w_out, g_post)


def _ffn_kernel(x_ref, gpre_ref, w1_ref, w2_ref, gpost_ref, o_ref):
    d_ff = w1_ref.shape[1]
    subs = [slice(r, r + SUB_TILE) for r in range(0, x_ref.shape[0], SUB_TILE)]
    h = [_rms(x_ref[rows, :], gpre_ref[...]).astype(BF16) for rows in subs]
    ff = [jnp.zeros((SUB_TILE, x_ref.shape[1]), F32) for _ in subs]
    for c in range(d_ff // FF_CHUNK):
        cols = slice(c * FF_CHUNK, (c + 1) * FF_CHUNK)
        for i in range(len(subs)):
            u = jnp.maximum(_dot(h[i], w1_ref[:, cols]), 0.0)
            ff[i] = ff[i] + _dot((u * u).astype(BF16), w2_ref[cols, :])
    for i, rows in enumerate(subs):
        o_ref[rows, :] = x_ref[rows, :] + _rms(ff[i], gpost_ref[...])


def _ffn(layer, x, g_pre, w1, w2, g_post):
    n, d = x.shape
    tm = TOKEN_TILE
    d_ff = w1.shape[2]
    return pl.pallas_call(
        _ffn_kernel,
        grid=(n // tm,),
        in_specs=[pl.BlockSpec((tm, d), lambda i: (i, 0)), _resident((1, d)),
                  _layer_resident(layer, (d, d_ff)), _layer_resident(layer, (d_ff, d)),
                  _resident((1, d))],
        out_specs=pl.BlockSpec((tm, d), lambda i: (i, 0)),
        out_shape=jax.ShapeDtypeStruct((n, d), F32),
        compiler_params=_params(1),
        name="ffn",
    )(x, g_pre, w1, w2, g_post)


def _block_diag(w):
    g, c, _ = w.shape
    eye = jnp.eye(g, dtype=w.dtype)
    return (eye[:, None, :, None] * w[:, :, None, :]).reshape(g * c, g * c)


def kernel(x, w_in, w_pool, pool_scale, gm_gain, w_spatial, b_spatial, w_br_sb, w_br_pool, w_br_gm,
           w_out, g_mix_pre, g_mix_post, g_ff_pre, g_ff_post, w_ff_in, w_ff_out):
    b, s, d = x.shape
    depth = w_in.shape[0]
    assert s % TOKEN_TILE == 0 and s % (ATTN_STEP_BLOCKS * ATTN_BLOCK) == 0
    assert TOKEN_TILE % GM_CHUNK == 0 and TOKEN_TILE % SUB_TILE == 0
    n = b * s
    xt = x.reshape(n, d)
    w_in, w_br_sb, w_br_pool, w_br_gm, w_out, w_ff_in, w_ff_out = (
        w.astype(BF16) for w in (w_in, w_br_sb, w_br_pool, w_br_gm, w_out, w_ff_in, w_ff_out))
    for l in range(depth):
        q, k, v, p_in, gu, gv, gates = _inproj(l, xt, g_mix_pre[l][None], gm_gain[l][None], w_in)
        o_sb = _attention(q.reshape(b, s, SB_WIDTH), k.reshape(b, s, SB_WIDTH),
                          v.reshape(b, s, SB_WIDTH)).reshape(n, SB_WIDTH)
        bias = jnp.repeat(b_spatial[l].T, GM_GROUP_DIM, axis=1)
        xt = _merge(l, xt, o_sb, p_in, gu, gv, gates, _block_diag(w_pool[l]).astype(BF16),
                    pool_scale[l][None], w_spatial[l], bias, w_br_sb, w_br_pool, w_br_gm, w_out,
                    g_mix_post[l][None], s)
        xt = _ffn(l, xt, g_ff_pre[l][None], w_ff_in, w_ff_out, g_ff_post[l][None])
    return xt.reshape(b, s, d)
```
